```python
import math
import jax, jax.numpy as jnp
from jax import lax
import numpy as np

D_MODEL = 2048
BATCH = 8
SEQ = 2048
DEPTH = 4

CHUNK = 64
PLE_DIM = 256
N_MIXERS = 2
SB_HEADS = 16
SB_HEAD_DIM = D_MODEL // SB_HEADS
SB_WIDTH = SB_HEADS * SB_HEAD_DIM
SB_QBLOCK = 128
SSM_EXPAND = 2
SSM_INNER = SSM_EXPAND * D_MODEL
SSM_HEAD_DIM = 64
SSM_HEADS = SSM_INNER // SSM_HEAD_DIM
SSM_GROUPS = 8
SSM_HEADS_PER_GROUP = SSM_HEADS // SSM_GROUPS
SSM_STATE = 128
SSM_CONV = 4
SSM_CONV_DIM = SSM_INNER + 2 * SSM_GROUPS * SSM_STATE
SSM_IN_PROJ = 2 * SSM_INNER + 2 * SSM_GROUPS * SSM_STATE + SSM_HEADS
D_FF = 4 * D_MODEL
NORM_EPS = 1e-6

kernel_name = 'hybrid_stickbreak_ssd_trunk'


def rms_norm(x, g):
    xf = x.astype(jnp.float32)
    y = xf * lax.rsqrt(jnp.mean(xf * xf, axis=-1, keepdims=True) + NORM_EPS)
    return (y * g.astype(jnp.float32)).astype(x.dtype)


def stick_breaking_attention(q, k, v):
    L = q.shape[2]
    scale = SB_HEAD_DIM ** -0.5
    outs = []
    for blk in range(L // SB_QBLOCK):
        t0 = blk * SB_QBLOCK
        t1 = t0 + SB_QBLOCK
        qb = q[:, :, t0:t1]
        kb = k[:, :, :t1]
        vb = v[:, :, :t1]
        z = jnp.einsum('bhtd,bhsd->bhts', qb, kb).astype(jnp.float32) * scale
        t_idx = t0 + jnp.arange(SB_QBLOCK)[:, None]
        s_idx = jnp.arange(t1)[None, :]
        mask = s_idx < t_idx
        log_beta = jax.nn.log_sigmoid(z)
        log_1m_beta = jnp.where(mask, -jax.nn.softplus(z), 0.0)
        suffix = lax.cumsum(log_1m_beta, axis=3, reverse=True) - log_1m_beta
        a = jnp.where(mask, jnp.exp(log_beta + suffix), 0.0)
        outs.append(jnp.einsum('bhts,bhsd->bhtd', a.astype(vb.dtype), vb))
    return jnp.concatenate(outs, axis=2)


def stick_breaking_mixer(hn, w_qkv, g_q, g_k, w_o):
    b, l, _ = hn.shape
    qkv = (hn @ w_qkv).reshape(b, l, 3, SB_HEADS, SB_HEAD_DIM)
    q = rms_norm(qkv[:, :, 0], g_q).transpose(0, 2, 1, 3)
    k = rms_norm(qkv[:, :, 1], g_k).transpose(0, 2, 1, 3)
    v = qkv[:, :, 2].transpose(0, 2, 1, 3)
    o = stick_breaking_attention(q, k, v)
    o = o.transpose(0, 2, 1, 3).reshape(b, l, SB_WIDTH)
    return o @ w_o


def causal_dwconv(x, w, bias):
    k = w.shape[0]
    y = lax.conv_general_dilated(x, w[:, None, :].astype(x.dtype), window_strides=(1,),
                                 padding=[(k - 1, 0)], dimension_numbers=('NWC', 'WIO', 'NWC'),
                                 feature_group_count=x.shape[-1])
    return y + bias.astype(x.dtype)


def ssd_chunked(x, dt, a, bm, cm):
    b, l, g, hg, pd = x.shape
    n = bm.shape[-1]
    c = l // CHUNK
    x = x.reshape(b, c, CHUNK, g, hg, pd)
    dt = dt.reshape(b, c, CHUNK, g, hg)
    bm = bm.reshape(b, c, CHUNK, g, n)
    cm = cm.reshape(b, c, CHUNK, g, n)
    acum = jnp.cumsum(dt * a, axis=2)
    diff = acum[:, :, :, None] - acum[:, :, None, :]
    causal = jnp.tril(jnp.ones((CHUNK, CHUNK), dtype=bool))[:, :, None, None]
    lmat = jnp.exp(jnp.where(causal, diff, -jnp.inf))
    cb = jnp.einsum('bcign,bcjgn->bcijg', cm, bm)
    y_diag = jnp.einsum('bcijg,bcijgh,bcjgh,bcjghp->bcighp', cb, lmat, dt, x)
    decay_states = jnp.exp(acum[:, :, -1:] - acum)
    states = jnp.einsum('bcjgn,bcjgh,bcjghp->bcghpn', bm, decay_states * dt, x)
    chunk_decay = jnp.exp(acum[:, :, -1])

    def step(carry, inp):
        st, dec = inp
        return carry * dec[..., None, None] + st, carry

    init = jnp.zeros((b, g, hg, pd, n), jnp.float32)
    _, h_in = lax.scan(step, init, (jnp.moveaxis(states, 1, 0).astype(jnp.float32),
                                    jnp.moveaxis(chunk_decay, 1, 0)))
    h_in = jnp.moveaxis(h_in, 0, 1)
    y_off = jnp.einsum('bcign,bcghpn,bcigh->bcighp', cm, h_in, jnp.exp(acum))
    return (y_diag + y_off).reshape(b, l, g, hg, pd)


def ssd_mixer(hn, w_in, conv_w, conv_b, dt_bias, a_log, d_skip, g_out, w_out):
    b, l, _ = hn.shape
    zxbcdt = hn @ w_in
    z = zxbcdt[..., :SSM_INNER]
    xbc = zxbcdt[..., SSM_INNER:SSM_INNER + SSM_CONV_DIM]
    dt_raw = zxbcdt[..., SSM_INNER + SSM_CONV_DIM:]
    xbc = jax.nn.silu(causal_dwconv(xbc, conv_w, conv_b))
    gn = SSM_GROUPS * SSM_STATE
    xs = xbc[..., :SSM_INNER].reshape(b, l, SSM_GROUPS, SSM_HEADS_PER_GROUP, SSM_HEAD_DIM)
    bm = xbc[..., SSM_INNER:SSM_INNER + gn].reshape(b, l, SSM_GROUPS, SSM_STATE)
    cm = xbc[..., SSM_INNER + gn:].reshape(b, l, SSM_GROUPS, SSM_STATE)
    dt = jax.nn.softplus(dt_raw.astype(jnp.float32) + dt_bias.astype(jnp.float32))
    dt = dt.reshape(b, l, SSM_GROUPS, SSM_HEADS_PER_GROUP)
    a = -jnp.exp(a_log.astype(jnp.float32)).reshape(SSM_GROUPS, SSM_HEADS_PER_GROUP)
    y = ssd_chunked(xs, dt, a, bm, cm)
    y = y + d_skip.astype(jnp.float32).reshape(SSM_GROUPS, SSM_HEADS_PER_GROUP, 1) * xs
    y = y.astype(hn.dtype).reshape(b, l, SSM_INNER) * jax.nn.silu(z)
    y = rms_norm(y.reshape(b, l, SSM_GROUPS, SSM_INNER // SSM_GROUPS),
                 g_out.reshape(SSM_GROUPS, SSM_INNER // SSM_GROUPS))
    return y.reshape(b, l, SSM_INNER) @ w_out


def squared_relu_mlp(hn, w_up, w_down):
    u = jax.nn.relu(hn @ w_up)
    return (u * u) @ w_down


def setup_inputs(seed: int = 0) -> dict:
    key = jax.random.key(seed)
    ks = jax.random.split(key, 24)
    na = (DEPTH + 1) // 2
    ns = DEPTH // 2
    res_scale = (2.0 * DEPTH) ** -0.5

    def dense(k, shape, fan_in, scale=1.0):
        return jax.random.normal(k, shape, jnp.float32) * (scale * fan_in ** -0.5)

    def gain(k, shape):
        return 1.0 + 0.05 * jax.random.normal(k, shape, jnp.float32)

    dt = jnp.exp(jax.random.uniform(ks[14], (ns, SSM_HEADS), jnp.float32)
                 * (math.log(0.1) - math.log(0.001)) + math.log(0.001))
    dt = jnp.maximum(dt, 1e-4)
    dt_bias = dt + jnp.log(-jnp.expm1(-dt))
    a_log = jnp.log(jax.random.uniform(ks[15], (ns, SSM_HEADS), jnp.float32, 1.0, 16.0))

    return {
        'x': jax.random.normal(ks[0], (BATCH, SEQ, D_MODEL), jnp.float32),
        'p': jax.random.normal(ks[1], (DEPTH, BATCH, SEQ, PLE_DIM), jnp.float32),
        'attn_norm': gain(ks[2], (na, D_MODEL)),
        'attn_w_qkv': dense(ks[3], (na, D_MODEL, 3 * SB_WIDTH), D_MODEL),
        'attn_q_norm': gain(ks[4], (na, SB_HEAD_DIM)),
        'attn_k_norm': gain(ks[5], (na, SB_HEAD_DIM)),
        'attn_w_o': dense(ks[6], (na, SB_WIDTH, D_MODEL), SB_WIDTH, res_scale),
        'ssm_norm': gain(ks[7], (ns, D_MODEL)),
        'ssm_w_in': dense(ks[8], (ns, D_MODEL, SSM_IN_PROJ), D_MODEL),
        'ssm_conv_w': dense(ks[9], (ns, SSM_CONV, SSM_CONV_DIM), SSM_CONV),
        'ssm_conv_b': 0.02 * jax.random.normal(ks[10], (ns, SSM_CONV_DIM), jnp.float32),
        'ssm_dt_bias': dt_bias,
        'ssm_a_log': a_log,
        'ssm_d': gain(ks[11], (ns, SSM_HEADS)),
        'ssm_out_norm': gain(ks[12], (ns, SSM_INNER)),
        'ssm_w_out': dense(ks[13], (ns, SSM_INNER, D_MODEL), SSM_INNER, res_scale),
        'mlp_norm': gain(ks[16], (DEPTH, D_MODEL)),
        'mlp_w_up': dense(ks[17], (DEPTH, D_MODEL, D_FF), D_MODEL),
        'mlp_w_down': dense(ks[18], (DEPTH, D_FF, D_MODEL), D_FF, res_scale),
        'ple_w_proj': dense(ks[19], (DEPTH, PLE_DIM, D_MODEL), PLE_DIM),
        'ple_norm': gain(ks[20], (DEPTH, D_MODEL)),
        'ple_gate_norm': gain(ks[21], (DEPTH, D_MODEL)),
        'ple_w_gate': dense(ks[22], (DEPTH, D_MODEL, D_MODEL), D_MODEL),
    }


def reference(x, p, attn_norm, attn_w_qkv, attn_q_norm, attn_k_norm, attn_w_o,
              ssm_norm, ssm_w_in, ssm_conv_w, ssm_conv_b, ssm_dt_bias, ssm_a_log, ssm_d,
              ssm_out_norm, ssm_w_out, mlp_norm, mlp_w_up, mlp_w_down,
              ple_w_proj, ple_norm, ple_gate_norm, ple_w_gate):
    h = x
    for i in range(DEPTH):
        j = i // N_MIXERS
        if i % N_MIXERS == 0:
            h = h + stick_breaking_mixer(rms_norm(h, attn_norm[j]), attn_w_qkv[j],
                                         attn_q_norm[j], attn_k_norm[j], attn_w_o[j])
        else:
            h = h + ssd_mixer(rms_norm(h, ssm_norm[j]), ssm_w_in[j], ssm_conv_w[j], ssm_conv_b[j],
                              ssm_dt_bias[j], ssm_a_log[j], ssm_d[j], ssm_out_norm[j], ssm_w_out[j])
        h = h + squared_relu_mlp(rms_norm(h, mlp_norm[i]), mlp_w_up[i], mlp_w_down[i])
        e = rms_norm(p[i] @ ple_w_proj[i], ple_norm[i])
        gate = jax.nn.sigmoid(rms_norm(h, ple_gate_norm[i]) @ ple_w_gate[i])
        h = h + gate * e
    return h
```

```python
import functools

import jax
import jax.numpy as jnp
from jax import lax
from jax.experimental import pallas as pl
from jax.experimental.pallas import tpu as pltpu

F32 = jnp.float32
BF16 = jnp.bfloat16

NORM_EPS = 1e-6
LANES = 128
VMEM_LIMIT = 48 * 1024 * 1024

DEPTH = 4
PLE_DIM = 256
SB_HEADS = 16
SB_HEAD_DIM = 128
SSM_HEAD_DIM = 64
SSM_GROUPS = 8
SSM_HEADS_PER_GROUP = 8
SSM_STATE = 128
SSM_CONV = 4
SSM_GROUP_WIDTH = SSM_HEADS_PER_GROUP * SSM_HEAD_DIM
SSD_CHUNK = 128
ATTN_BLOCK = 256
NEG_BIG = -1e30


def _params(*sem):
    return pltpu.CompilerParams(dimension_semantics=sem, vmem_limit_bytes=VMEM_LIMIT)


def _rms(x, g):
    ms = jnp.mean(x * x, axis=-1, keepdims=True)
    return x * lax.rsqrt(ms + NORM_EPS) * g


def _softplus(x):
    return jnp.maximum(x, 0.0) + jnp.log(1.0 + jnp.exp(-jnp.abs(x)))


def _silu(x):
    return x * jax.nn.sigmoid(x)


def _split2(x):
    hi = x.astype(BF16)
    lo = (x - hi.astype(F32)).astype(BF16)
    return hi, lo


def _split3(x):
    hi = x.astype(BF16)
    r1 = x - hi.astype(F32)
    mid = r1.astype(BF16)
    lo = (r1 - mid.astype(F32)).astype(BF16)
    return hi, mid, lo


def _dot(a, b):
    return jnp.dot(a, b, preferred_element_type=F32)


def _dot_nt(a, b):
    return lax.dot_general(a, b, (((1,), (1,)), ((), ())), preferred_element_type=F32)


def _dot_tn(a, b):
    return lax.dot_general(a, b, (((0,), (0,)), ((), ())), preferred_element_type=F32)


def _norm_mm_kernel(h_ref, g_ref, w_ref, o_ref, hn_ref, *, act):
    @pl.when(pl.program_id(1) == 0)
    def _():
        hn_ref[...] = _rms(h_ref[...], g_ref[...]).astype(BF16)

    acc = _dot(hn_ref[...], w_ref[...])
    if act == "relu2":
        r = jnp.maximum(acc, 0.0)
        acc = r * r
    o_ref[...] = acc.astype(o_ref.dtype)


def _norm_mm(h, g, w, *, act, out_dtype, tm, tn):
    m, d = h.shape
    n = w.shape[1]
    tm, tn = min(tm, m), min(tn, n)
    return pl.pallas_call(
        functools.partial(_norm_mm_kernel, act=act),
        grid=(m // tm, n // tn),
        in_specs=[
            pl.BlockSpec((tm, d), lambda i, j: (i, 0)),
            pl.BlockSpec((1, d), lambda i, j: (0, 0)),
            pl.BlockSpec((d, tn), lambda i, j: (0, j)),
        ],
        out_specs=pl.BlockSpec((tm, tn), lambda i, j: (i, j)),
        out_shape=jax.ShapeDtypeStruct((m, n), out_dtype),
        scratch_shapes=[pltpu.VMEM((tm, d), BF16)],
        compiler_params=_params("parallel", "arbitrary"),
        name="norm_mm_" + (act or "plain"),
    )(h, g.reshape(1, d), w)


def _qkv_kernel(h_ref, g_ref, w_ref, gq_ref, gk_ref, o_ref, hn_ref, *, tn, q_tiles, k_tiles):
    j = pl.program_id(1)

    @pl.when(j == 0)
    def _():
        hn_ref[...] = _rms(h_ref[...], g_ref[...]).astype(BF16)

    acc = _dot(hn_ref[...], w_ref[...])

    @pl.when(j < k_tiles)
    def _():
        gh = jnp.where(j < q_tiles, gq_ref[...], gk_ref[...])
        for hh in range(tn // SB_HEAD_DIM):
            sl = slice(hh * SB_HEAD_DIM, (hh + 1) * SB_HEAD_DIM)
            o_ref[:, sl] = _rms(acc[:, sl], gh).astype(BF16)

    @pl.when(j >= k_tiles)
    def _():
        o_ref[...] = acc.astype(BF16)


def _qkv_mm(h, g, w, gq, gk, *, tm, tn):
    m, d = h.shape
    n = w.shape[1]
    width = n // 3
    tm = min(tm, m)
    assert width % tn == 0
    return pl.pallas_call(
        functools.partial(_qkv_kernel, tn=tn, q_tiles=width // tn, k_tiles=2 * width // tn),
        grid=(m // tm, n // tn),
        in_specs=[
            pl.BlockSpec((tm, d), lambda i, j: (i, 0)),
            pl.BlockSpec((1, d), lambda i, j: (0, 0)),
            pl.BlockSpec((d, tn), lambda i, j: (0, j)),
            pl.BlockSpec((1, SB_HEAD_DIM), lambda i, j: (0, 0)),
            pl.BlockSpec((1, SB_HEAD_DIM), lambda i, j: (0, 0)),
        ],
        out_specs=pl.BlockSpec((tm, tn), lambda i, j: (i, j)),
        out_shape=jax.ShapeDtypeStruct((m, n), BF16),
        scratch_shapes=[pltpu.VMEM((tm, d), BF16)],
        compiler_params=_params("parallel", "arbitrary"),
        name="qkv_mm",
    )(h, g.reshape(1, d), w, gq.reshape(1, -1), gk.reshape(1, -1))


def _res_mm_kernel(a_ref, w_ref, r_ref, o_ref, acc_ref, *, nk):
    d = _dot(a_ref[...], w_ref[...])
    if nk == 1:
        o_ref[...] = r_ref[...] + d
        return
    k = pl.program_id(2)

    @pl.when(k == 0)
    def _():
        acc_ref[...] = d

    @pl.when((k > 0) & (k < nk - 1))
    def _():
        acc_ref[...] += d

    @pl.when(k == nk - 1)
    def _():
        o_ref[...] = r_ref[...] + (acc_ref[...] + d)


def _res_mm(a, w, res, *, tm, tn, tk):
    m, kdim = a.shape
    n = w.shape[1]
    tm, tn, tk = min(tm, m), min(tn, n), min(tk, kdim)
    nk = kdim // tk
    return pl.pallas_call(
        functools.partial(_res_mm_kernel, nk=nk),
        grid=(m // tm, n // tn, nk),
        in_specs=[
            pl.BlockSpec((tm, tk), lambda i, j, k: (i, k)),
            pl.BlockSpec((tk, tn), lambda i, j, k: (k, j)),
            pl.BlockSpec((tm, tn), lambda i, j, k: (i, j)),
        ],
        out_specs=pl.BlockSpec((tm, tn), lambda i, j, k: (i, j)),
        out_shape=jax.ShapeDtypeStruct((m, n), F32),
        scratch_shapes=[pltpu.VMEM((tm, tn), F32)],
        compiler_params=_params("parallel", "parallel", "arbitrary"),
        name="res_mm",
    )(a, w, res)


def _ple_kernel(h_ref, p_ref, gg_ref, wg_ref, wp_ref, gp_ref, o_ref, hn_ref, e_ref, *, tn):
    j = pl.program_id(1)

    @pl.when(j == 0)
    def _():
        hn_ref[...] = _rms(h_ref[...], gg_ref[...]).astype(BF16)
        e_ref[...] = _rms(_dot(p_ref[...].astype(BF16), wp_ref[...]), gp_ref[...])

    gate = jax.nn.sigmoid(_dot(hn_ref[...], wg_ref[...]))
    col = pl.multiple_of(j * tn, tn)
    o_ref[...] = h_ref[:, pl.ds(col, tn)] + gate * e_ref[:, pl.ds(col, tn)]


def _ple(h, p, gg, wg, wp, gp, *, tm, tn):
    m, d = h.shape
    pd = p.shape[1]
    tm, tn = min(tm, m), min(tn, d)
    return pl.pallas_call(
        functools.partial(_ple_kernel, tn=tn),
        grid=(m // tm, d // tn),
        in_specs=[
            pl.BlockSpec((tm, d), lambda i, j: (i, 0)),
            pl.BlockSpec((tm, pd), lambda i, j: (i, 0)),
            pl.BlockSpec((1, d), lambda i, j: (0, 0)),
            pl.BlockSpec((d, tn), lambda i, j: (0, j)),
            pl.BlockSpec((pd, d), lambda i, j: (0, 0)),
            pl.BlockSpec((1, d), lambda i, j: (0, 0)),
        ],
        out_specs=pl.BlockSpec((tm, tn), lambda i, j: (i, j)),
        out_shape=jax.ShapeDtypeStruct((m, d), F32),
        scratch_shapes=[pltpu.VMEM((tm, d), BF16), pltpu.VMEM((tm, d), F32)],
        compiler_params=_params("parallel", "arbitrary"),
        name="ple",
    )(h, p, gg.reshape(1, d), wg, wp, gp.reshape(1, d))


def _attn_kernel(q_ref, k_ref, v_ref, t_ref, o_ref, *, blk, scale):
    qi = pl.program_id(2)
    q = q_ref[...]
    tri = t_ref[...]

    def tile(kb, carry, masked):
        start = pl.multiple_of(kb * blk, blk)
        k = k_ref[pl.ds(start, blk), :]
        v = v_ref[pl.ds(start, blk), :]
        z = _dot_nt(q, k) * scale
        sp = _softplus(z)
        l1m = -sp
        if masked:
            t_idx = lax.broadcasted_iota(jnp.int32, (blk, blk), 0)
            s_idx = lax.broadcasted_iota(jnp.int32, (blk, blk), 1)
            mask = s_idx < t_idx
            l1m = jnp.where(mask, l1m, 0.0)
        hi, lo = _split2(l1m)
        suffix = _dot(hi, tri) + _dot(lo, tri)
        a = jnp.exp((z - sp) + suffix + carry)
        if masked:
            a = jnp.where(mask, a, 0.0)
        pv = _dot(a.astype(BF16), v)
        return pv, carry + jnp.sum(l1m, axis=-1, keepdims=True)

    acc, carry = tile(qi, jnp.zeros((blk, 1), F32), True)

    def body(it, state):
        acc, carry = state
        pv, carry = tile(qi - 1 - it, carry, False)
        return acc + pv, carry

    acc, _ = lax.fori_loop(0, qi, body, (acc, carry))
    o_ref[...] = acc.astype(o_ref.dtype)


def _attention(qkv, batch, seq):
    blk = min(ATTN_BLOCK, seq)
    nq = seq // blk
    h = SB_HEADS
    tri = (lax.broadcasted_iota(jnp.int32, (blk, blk), 0)
           > lax.broadcasted_iota(jnp.int32, (blk, blk), 1)).astype(BF16)
    return pl.pallas_call(
        functools.partial(_attn_kernel, blk=blk, scale=SB_HEAD_DIM ** -0.5),
        grid=(batch, h, nq),
        in_specs=[
            pl.BlockSpec((blk, SB_HEAD_DIM), lambda b, hh, i: (b * nq + i, hh)),
            pl.BlockSpec((seq, SB_HEAD_DIM), lambda b, hh, i: (b, h + hh)),
            pl.BlockSpec((seq, SB_HEAD_DIM), lambda b, hh, i: (b, 2 * h + hh)),
            pl.BlockSpec((blk, blk), lambda b, hh, i: (0, 0)),
        ],
        out_specs=pl.BlockSpec((blk, SB_HEAD_DIM), lambda b, hh, i: (b * nq + i, hh)),
        out_shape=jax.ShapeDtypeStruct((batch * seq, h * SB_HEAD_DIM), BF16),
        compiler_params=_params("parallel", "parallel", "arbitrary"),
        name="stickbreak_attn",
    )(qkv, qkv, qkv, tri)


def _conv_silu(x_ref, prev_ref, w_ref, b_ref, first_chunk):
    x = x_ref[...]
    prev = jnp.where(first_chunk, 0.0, prev_ref[...])
    ext = jnp.concatenate([prev, x], axis=0)
    rows = x.shape[0]
    acc = x * w_ref[SSM_CONV - 1:SSM_CONV, :] + b_ref[...]
    for s in range(1, SSM_CONV):
        acc = acc + ext[8 - s:8 - s + rows, :] * w_ref[SSM_CONV - 1 - s:SSM_CONV - s, :]
    return _silu(acc)


def _ssd_kernel(z_ref, x_ref, b_ref, c_ref, dt_ref, px_ref, pb_ref, pc_ref,
                wx_ref, wb_ref, wc_ref, bx_ref, bb_ref, bc_ref,
                dtb_ref, alog_ref, d_ref, g_ref, tri_ref, o_ref, state_ref, *, q):
    c = pl.program_id(2)
    first = c == 0

    @pl.when(first)
    def _():
        state_ref[...] = jnp.zeros_like(state_ref)

    xg = _conv_silu(x_ref, px_ref, wx_ref, bx_ref, first)
    bm = _conv_silu(b_ref, pb_ref, wb_ref, bb_ref, first)
    cm = _conv_silu(c_ref, pc_ref, wc_ref, bc_ref, first)
    bm16, cm16 = bm.astype(BF16), cm.astype(BF16)

    dt = _softplus(dt_ref[...] + dtb_ref[...])
    da = dt * (-jnp.exp(alog_ref[...]))
    tri = tri_ref[...]
    hi, mid, lo = _split3(da)
    acum = _dot(tri, hi) + _dot(tri, mid) + _dot(tri, lo)
    acum_t = acum.T
    dt_t = dt.T
    grow = jnp.exp(acum)
    tail = jnp.exp(acum[q - 1:q, :] - acum) * dt

    i_idx = lax.broadcasted_iota(jnp.int32, (q, q), 0)
    j_idx = lax.broadcasted_iota(jnp.int32, (q, q), 1)
    causal = j_idx <= i_idx
    lane = lax.broadcasted_iota(jnp.int32, (q, LANES), 1)
    left = lane < SSM_HEAD_DIM
    left2 = lax.broadcasted_iota(jnp.int32, (2 * q, LANES), 1) < SSM_HEAD_DIM
    top2 = lax.broadcasted_iota(jnp.int32, (2 * q, LANES), 0) < q

    cb = _dot_nt(cm16, bm16)
    state = state_ref[...]
    y_off = _dot(cm16, state.astype(BF16))

    y_parts, xw_parts, grow_parts = [], [], []
    for pair in range(SSM_HEADS_PER_GROUP // 2):
        h0, h1 = 2 * pair, 2 * pair + 1
        sl = slice(pair * LANES, (pair + 1) * LANES)
        mats = []
        for hh in (h0, h1):
            diff = acum[:, hh:hh + 1] - acum_t[hh:hh + 1, :]
            decay = jnp.exp(jnp.where(causal, diff, NEG_BIG))
            mats.append((cb * decay * dt_t[hh:hh + 1, :]).astype(BF16))
        m_cat = jnp.concatenate(mats, axis=1)
        xp = xg[:, sl]
        x2 = jnp.concatenate([xp, xp], axis=0)
        x_bd = jnp.where(left2 == top2, x2, 0.0).astype(BF16)
        y_diag = _dot(m_cat, x_bd)
        grow_p = jnp.where(left, grow[:, h0:h0 + 1], grow[:, h1:h1 + 1])
        tail_p = jnp.where(left, tail[:, h0:h0 + 1], tail[:, h1:h1 + 1])
        y_parts.append(y_diag + y_off[:, sl] * grow_p + d_ref[:, sl] * xp)
        xw_parts.append((xp * tail_p).astype(BF16))
        grow_parts.append(grow_p[q - 1:q, :])

    xw = jnp.concatenate(xw_parts, axis=1)
    chunk_decay = jnp.concatenate(grow_parts, axis=1)
    state_ref[...] = state * chunk_decay + _dot_tn(bm16, xw)

    y = jnp.concatenate(y_parts, axis=1)
    o_ref[...] = _rms(y * _silu(z_ref[...]), g_ref[...]).astype(o_ref.dtype)


def _ssd(zx, batch, seq, conv_w, conv_b, dt_bias, a_log, d_skip, g_out):
    q = min(SSD_CHUNK, seq)
    nc = seq // q
    gw, ns, g = SSM_GROUP_WIDTH, SSM_STATE, SSM_GROUPS
    inner = g * gw
    x0, b0, c0, dt0 = inner // gw, 2 * inner // ns, (2 * inner + g * ns) // ns, (2 * inner + 2 * g * ns) // LANES

    def row(b, gg, c):
        return b * nc + c

    def prow(b, gg, c):
        return jnp.maximum((b * nc + c) * (q // 8) - 1, 0)

    tri = (lax.broadcasted_iota(jnp.int32, (q, q), 1)
           <= lax.broadcasted_iota(jnp.int32, (q, q), 0)).astype(BF16)
    cw = conv_w
    cbias = conv_b.reshape(1, -1)
    in_specs = [
        pl.BlockSpec((q, gw), lambda b, gg, c: (row(b, gg, c), gg)),
        pl.BlockSpec((q, gw), lambda b, gg, c: (row(b, gg, c), x0 + gg)),
        pl.BlockSpec((q, ns), lambda b, gg, c: (row(b, gg, c), b0 + gg)),
        pl.BlockSpec((q, ns), lambda b, gg, c: (row(b, gg, c), c0 + gg)),
        pl.BlockSpec((q, LANES), lambda b, gg, c: (row(b, gg, c), dt0 + gg)),
        pl.BlockSpec((8, gw), lambda b, gg, c: (prow(b, gg, c), x0 + gg)),
        pl.BlockSpec((8, ns), lambda b, gg, c: (prow(b, gg, c), b0 + gg)),
        pl.BlockSpec((8, ns), lambda b, gg, c: (prow(b, gg, c), c0 + gg)),
        pl.BlockSpec((SSM_CONV, gw), lambda b, gg, c: (0, gg)),
        pl.BlockSpec((SSM_CONV, ns), lambda b, gg, c: (0, inner // ns + gg)),
        pl.BlockSpec((SSM_CONV, ns), lambda b, gg, c: (0, inner // ns + g + gg)),
        pl.BlockSpec((1, gw), lambda b, gg, c: (0, gg)),
        pl.BlockSpec((1, ns), lambda b, gg, c: (0, inner // ns + gg)),
        pl.BlockSpec((1, ns), lambda b, gg, c: (0, inner // ns + g + gg)),
        pl.BlockSpec((1, LANES), lambda b, gg, c: (0, gg)),
        pl.BlockSpec((1, LANES), lambda b, gg, c: (0, gg)),
        pl.BlockSpec((1, gw), lambda b, gg, c: (0, gg)),
        pl.BlockSpec((1, gw), lambda b, gg, c: (0, gg)),
        pl.BlockSpec((q, q), lambda b, gg, c: (0, 0)),
    ]
    return pl.pallas_call(
        functools.partial(_ssd_kernel, q=q),
        grid=(batch, g, nc),
        in_specs=in_specs,
        out_specs=pl.BlockSpec((q, gw), lambda b, gg, c: (row(b, gg, c), gg)),
        out_shape=jax.ShapeDtypeStruct((batch * seq, inner), BF16),
        scratch_shapes=[pltpu.VMEM((ns, gw), F32)],
        compiler_params=_params("parallel", "parallel", "arbitrary"),
        name="ssd",
    )(zx, zx, zx, zx, zx, zx, zx, zx, cw, cw, cw, cbias, cbias, cbias,
      dt_bias, a_log, d_skip, g_out.reshape(1, -1), tri)


def _pad_heads(v):
    v = v.reshape(SSM_GROUPS, SSM_HEADS_PER_GROUP)
    v = jnp.pad(v, ((0, 0), (0, LANES - SSM_HEADS_PER_GROUP)))
    return v.reshape(1, SSM_GROUPS * LANES)


def _pad_dt_columns(w_in):
    d = w_in.shape[0]
    heads = SSM_GROUPS * SSM_HEADS_PER_GROUP
    main, w_dt = w_in[:, :-heads], w_in[:, -heads:]
    w_dt = w_dt.reshape(d, SSM_GROUPS, SSM_HEADS_PER_GROUP)
    w_dt = jnp.pad(w_dt, ((0, 0), (0, 0), (0, LANES - SSM_HEADS_PER_GROUP)))
    return jnp.concatenate([main, w_dt.reshape(d, SSM_GROUPS * LANES)], axis=1)


def kernel(x, p, attn_norm, attn_w_qkv, attn_q_norm, attn_k_norm, attn_w_o, ssm_norm, ssm_w_in, ssm_conv_w, ssm_conv_b, ssm_dt_bias, ssm_a_log, ssm_d, ssm_out_norm, ssm_w_out, mlp_norm, mlp_w_up, mlp_w_down, ple_w_proj, ple_norm, ple_gate_norm, ple_w_gate):
    batch, seq, d = x.shape
    m = batch * seq
    h = x.reshape(m, d)
    for i in range(DEPTH):
        j = i // 2
        if i % 2 == 0:
            qkv = _qkv_mm(h, attn_norm[j], attn_w_qkv[j].astype(BF16),
                          attn_q_norm[j], attn_k_norm[j], tm=1024, tn=1024)
            o = _attention(qkv, batch, seq)
            h = _res_mm(o, attn_w_o[j].astype(BF16), h, tm=1024, tn=1024, tk=2048)
        else:
            w_in = _pad_dt_columns(ssm_w_in[j]).astype(BF16)
            zx = _norm_mm(h, ssm_norm[j], w_in, act=None, out_dtype=F32, tm=1024, tn=1024)
            y = _ssd(zx, batch, seq, ssm_conv_w[j], ssm_conv_b[j],
                     _pad_heads(ssm_dt_bias[j]), _pad_heads(ssm_a_log[j]),
                     jnp.repeat(ssm_d[j], SSM_HEAD_DIM).reshape(1, -1), ssm_out_norm[j])
            h = _res_mm(y, ssm_w_out[j].astype(BF16), h, tm=1024, tn=1024, tk=2048)
        u = _norm_mm(h, mlp_norm[i], mlp_w_up[i].astype(BF16), act="relu2", out_dtype=BF16,
                     tm=1024, tn=1024)
        h = _res_mm(u, mlp_w_down[i].astype(BF16), h, tm=1024, tn=1024, tk=2048)
        h = _ple(h, p[i].reshape(m, PLE_DIM), ple_gate_norm[i], ple_w_gate[i].astype(BF16),
                 ple_w_proj[i].astype(BF16), ple_norm[i], tm=512, tn=1024)
    return h.reshape(batch, seq, d)
```

```python
import functools

import jax
import jax.numpy as jnp
from jax import lax
from jax.experimental import pallas as pl
from jax.experimental.pallas import tpu as pltpu

F32 = jnp.float32
BF16 = jnp.bfloat16

NORM_EPS = 1e-6
LANES = 128
VMEM_LIMIT = 48 * 1024 * 1024

DEPTH = 4
PLE_DIM = 256
SB_HEADS = 16
SB_HEAD_DIM = 128
SSM_HEAD_DIM = 64
SSM_GROUPS = 8
SSM_HEADS_PER_GROUP = 8
SSM_STATE = 128
SSM_CONV = 4
SSM_GROUP_WIDTH = SSM_HEADS_PER_GROUP * SSM_HEAD_DIM
SSD_CHUNK = 128
SSD_GROUPS_PER_STEP = 4
ATTN_BLOCK = 256
ATTN_HEADS_PER_STEP = 4
LOG2E = 1.4426950408889634
NEG_BIG = -1e30


def _params(*sem):
    return pltpu.CompilerParams(dimension_semantics=sem, vmem_limit_bytes=VMEM_LIMIT)


def _rms(x, g):
    ms = jnp.mean(x * x, axis=-1, keepdims=True)
    return x * lax.rsqrt(ms + NORM_EPS) * g


def _softplus(x):
    return jnp.maximum(x, 0.0) + jnp.log(1.0 + jnp.exp(-jnp.abs(x)))


def _softplus2(x):
    neg_abs = lax.bitcast_convert_type(
        lax.bitcast_convert_type(x, jnp.uint32) | jnp.uint32(0x80000000), F32)
    return jnp.maximum(x, 0.0) + jnp.log2(1.0 + jnp.exp2(neg_abs))


def _silu(x):
    return x * jax.nn.sigmoid(x)


def _split2(x):
    hi = x.astype(BF16)
    lo = (x - hi.astype(F32)).astype(BF16)
    return hi, lo


def _split3(x):
    hi = x.astype(BF16)
    r1 = x - hi.astype(F32)
    mid = r1.astype(BF16)
    lo = (r1 - mid.astype(F32)).astype(BF16)
    return hi, mid, lo


def _dot(a, b):
    return jnp.dot(a, b, preferred_element_type=F32)


def _dot_nt(a, b):
    return lax.dot_general(a, b, (((1,), (1,)), ((), ())), preferred_element_type=F32)


def _dot_tn(a, b):
    return lax.dot_general(a, b, (((0,), (0,)), ((), ())), preferred_element_type=F32)


def _norm_mm_kernel(h_ref, g_ref, w_ref, b_ref, o_ref, hn_ref, *, act):
    @pl.when(pl.program_id(1) == 0)
    def _():
        hn_ref[...] = _rms(h_ref[...], g_ref[...]).astype(BF16)

    acc = _dot(hn_ref[...], w_ref[...])
    if act == "relu2":
        r = jnp.maximum(acc, 0.0)
        acc = r * r
    elif act == "silu":
        acc = _silu(acc)
    elif act == "softplus_bias":
        acc = _softplus(acc + b_ref[...])
    o_ref[...] = acc.astype(o_ref.dtype)


def _norm_mm(h, g, w, layer, *, act, out_dtype, tm, tn, col0=0, ncols=None, bias=None):
    m, d = h.shape
    n = w.shape[2] if ncols is None else ncols
    tm, tn = min(tm, m), min(tn, n)
    assert col0 % tn == 0 and n % tn == 0
    if bias is None:
        bias = jnp.zeros((1, n), F32)
    return pl.pallas_call(
        functools.partial(_norm_mm_kernel, act=act),
        grid=(m // tm, n // tn),
        in_specs=[
            pl.BlockSpec((tm, d), lambda i, j: (i, 0)),
            pl.BlockSpec((1, d), lambda i, j: (0, 0)),
            pl.BlockSpec((None, d, tn), lambda i, j: (layer, 0, col0 // tn + j)),
            pl.BlockSpec((1, tn), lambda i, j: (0, j)),
        ],
        out_specs=pl.BlockSpec((tm, tn), lambda i, j: (i, j)),
        out_shape=jax.ShapeDtypeStruct((m, n), out_dtype),
        scratch_shapes=[pltpu.VMEM((tm, d), BF16)],
        compiler_params=_params("parallel", "arbitrary"),
        name="norm_mm_" + act,
    )(h, g.reshape(1, d), w, bias)


def _qkv_kernel(h_ref, g_ref, w_ref, gq_ref, gk_ref, o_ref, hn_ref, *, tn, q_tiles, k_tiles):
    j = pl.program_id(1)

    @pl.when(j == 0)
    def _():
        hn_ref[...] = _rms(h_ref[...], g_ref[...]).astype(BF16)

    acc = _dot(hn_ref[...], w_ref[...])

    @pl.when(j < k_tiles)
    def _():
        gh = jnp.where(j < q_tiles, gq_ref[...], gk_ref[...])
        for hh in range(tn // SB_HEAD_DIM):
            sl = slice(hh * SB_HEAD_DIM, (hh + 1) * SB_HEAD_DIM)
            o_ref[:, sl] = _rms(acc[:, sl], gh).astype(BF16)

    @pl.when(j >= k_tiles)
    def _():
        o_ref[...] = acc.astype(BF16)


def _qkv_mm(h, g, w, layer, gq, gk, *, tm, tn):
    m, d = h.shape
    n = w.shape[2]
    width = n // 3
    tm = min(tm, m)
    assert width % tn == 0
    return pl.pallas_call(
        functools.partial(_qkv_kernel, tn=tn, q_tiles=width // tn, k_tiles=2 * width // tn),
        grid=(m // tm, n // tn),
        in_specs=[
            pl.BlockSpec((tm, d), lambda i, j: (i, 0)),
            pl.BlockSpec((1, d), lambda i, j: (0, 0)),
            pl.BlockSpec((None, d, tn), lambda i, j: (layer, 0, j)),
            pl.BlockSpec((1, SB_HEAD_DIM), lambda i, j: (0, 0)),
            pl.BlockSpec((1, SB_HEAD_DIM), lambda i, j: (0, 0)),
        ],
        out_specs=pl.BlockSpec((tm, tn), lambda i, j: (i, j)),
        out_shape=jax.ShapeDtypeStruct((m, n), BF16),
        scratch_shapes=[pltpu.VMEM((tm, d), BF16)],
        compiler_params=_params("parallel", "arbitrary"),
        name="qkv_mm",
    )(h, g.reshape(1, d), w, gq.reshape(1, -1), gk.reshape(1, -1))


def _res_mm_kernel(a_ref, w_ref, r_ref, o_ref, acc_ref, *, nk):
    d = _dot(a_ref[...], w_ref[...])
    if nk == 1:
        o_ref[...] = r_ref[...] + d
        return
    k = pl.program_id(2)

    @pl.when(k == 0)
    def _():
        acc_ref[...] = d

    @pl.when((k > 0) & (k < nk - 1))
    def _():
        acc_ref[...] += d

    @pl.when(k == nk - 1)
    def _():
        o_ref[...] = r_ref[...] + (acc_ref[...] + d)


def _res_mm(a, w, layer, res, *, tm, tn, tk):
    m, kdim = a.shape
    n = w.shape[2]
    tm, tn, tk = min(tm, m), min(tn, n), min(tk, kdim)
    nk = kdim // tk
    return pl.pallas_call(
        functools.partial(_res_mm_kernel, nk=nk),
        grid=(m // tm, n // tn, nk),
        in_specs=[
            pl.BlockSpec((tm, tk), lambda i, j, k: (i, k)),
            pl.BlockSpec((None, tk, tn), lambda i, j, k: (layer, k, j)),
            pl.BlockSpec((tm, tn), lambda i, j, k: (i, j)),
        ],
        out_specs=pl.BlockSpec((tm, tn), lambda i, j, k: (i, j)),
        out_shape=jax.ShapeDtypeStruct((m, n), F32),
        scratch_shapes=[pltpu.VMEM((tm, tn), F32)],
        compiler_params=_params("parallel", "parallel", "arbitrary"),
        name="res_mm",
    )(a, w, res)


def _ple_kernel(h_ref, p_ref, gg_ref, wg_ref, wp_ref, gp_ref, o_ref, hn_ref, e_ref, *, tn):
    j = pl.program_id(1)

    @pl.when(j == 0)
    def _():
        hn_ref[...] = _rms(h_ref[...], gg_ref[...]).astype(BF16)
        e_ref[...] = _rms(_dot(p_ref[...].astype(BF16), wp_ref[...]), gp_ref[...])

    gate = jax.nn.sigmoid(_dot(hn_ref[...], wg_ref[...]))
    col = pl.multiple_of(j * tn, tn)
    o_ref[...] = h_ref[:, pl.ds(col, tn)] + gate * e_ref[:, pl.ds(col, tn)]


def _ple(h, p, layer, gg, wg, wp, gp, *, tm, tn):
    m, d = h.shape
    pd = p.shape[2]
    tm, tn = min(tm, m), min(tn, d)
    return pl.pallas_call(
        functools.partial(_ple_kernel, tn=tn),
        grid=(m // tm, d // tn),
        in_specs=[
            pl.BlockSpec((tm, d), lambda i, j: (i, 0)),
            pl.BlockSpec((None, tm, pd), lambda i, j: (layer, i, 0)),
            pl.BlockSpec((1, d), lambda i, j: (0, 0)),
            pl.BlockSpec((None, d, tn), lambda i, j: (layer, 0, j)),
            pl.BlockSpec((None, pd, d), lambda i, j: (layer, 0, 0)),
            pl.BlockSpec((1, d), lambda i, j: (0, 0)),
        ],
        out_specs=pl.BlockSpec((tm, tn), lambda i, j: (i, j)),
        out_shape=jax.ShapeDtypeStruct((m, d), F32),
        scratch_shapes=[pltpu.VMEM((tm, d), BF16), pltpu.VMEM((tm, d), F32)],
        compiler_params=_params("parallel", "arbitrary"),
        name="ple",
    )(h, p, gg.reshape(1, d), wg, wp, gp.reshape(1, d))


def _attn_kernel(q_ref, k_ref, v_ref, t_ref, o_ref, *, blk, heads, scale):
    qi = pl.program_id(2)
    dh = SB_HEAD_DIM
    tri2 = t_ref[...]
    qs = [q_ref[:, hh * dh:(hh + 1) * dh] for hh in range(heads)]

    def tiles(kb, carries, masked):
        start = pl.multiple_of(kb * blk, blk)
        hs = range(heads)
        ks = [k_ref[pl.ds(start, blk), hh * dh:(hh + 1) * dh] for hh in hs]
        zs = [_dot_nt(qs[hh], ks[hh]) * (scale * LOG2E) for hh in hs]
        sps = [_softplus2(z) for z in zs]
        spms = sps
        if masked:
            t_idx = lax.broadcasted_iota(jnp.int32, (blk, blk), 0)
            s_idx = lax.broadcasted_iota(jnp.int32, (blk, blk), 1)
            mask = s_idx < t_idx
            spms = [jnp.where(mask, sp, 0.0) for sp in sps]
        splits = [_split2(spm) for spm in spms]
        sufs = [_dot(jnp.concatenate([hi, lo], axis=1), tri2) for hi, lo in splits]
        a_s = [jnp.exp2((zs[hh] - sps[hh]) - sufs[hh] + carries[hh]) for hh in hs]
        if masked:
            a_s = [jnp.where(mask, a, 0.0) for a in a_s]
        vs = [v_ref[pl.ds(start, blk), hh * dh:(hh + 1) * dh] for hh in hs]
        pvs = [_dot(a_s[hh].astype(BF16), vs[hh]) for hh in hs]
        new_carries = [carries[hh] - jnp.sum(spms[hh], axis=-1, keepdims=True) for hh in hs]
        return pvs, new_carries

    accs, carries = tiles(qi, [jnp.zeros((blk, 1), F32)] * heads, True)

    def body(it, state):
        accs, carries = state[:heads], state[heads:]
        pvs, carries = tiles(qi - 1 - it, carries, False)
        return tuple(acc + pv for acc, pv in zip(accs, pvs)) + tuple(carries)

    state = lax.fori_loop(0, qi, body, tuple(accs) + tuple(carries))
    for hh in range(heads):
        o_ref[:, hh * dh:(hh + 1) * dh] = state[hh].astype(o_ref.dtype)


def _attention(qkv, batch, seq):
    blk = min(ATTN_BLOCK, seq)
    nq = seq // blk
    hg = SB_HEADS // ATTN_HEADS_PER_STEP
    gw = ATTN_HEADS_PER_STEP * SB_HEAD_DIM
    tri = (lax.broadcasted_iota(jnp.int32, (blk, blk), 0)
           > lax.broadcasted_iota(jnp.int32, (blk, blk), 1)).astype(BF16)
    tri2 = jnp.concatenate([tri, tri], axis=0)
    return pl.pallas_call(
        functools.partial(_attn_kernel, blk=blk, heads=ATTN_HEADS_PER_STEP, scale=SB_HEAD_DIM ** -0.5),
        grid=(batch, hg, nq),
        in_specs=[
            pl.BlockSpec((blk, gw), lambda b, g, i: (b * nq + i, g)),
            pl.BlockSpec((seq, gw), lambda b, g, i: (b, hg + g)),
            pl.BlockSpec((seq, gw), lambda b, g, i: (b, 2 * hg + g)),
            pl.BlockSpec((2 * blk, blk), lambda b, g, i: (0, 0)),
        ],
        out_specs=pl.BlockSpec((blk, gw), lambda b, g, i: (b * nq + i, g)),
        out_shape=jax.ShapeDtypeStruct((batch * seq, SB_HEADS * SB_HEAD_DIM), BF16),
        compiler_params=_params("parallel", "parallel", "arbitrary"),
        name="stickbreak_attn",
    )(qkv, qkv, qkv, tri2)


def _conv_mm_kernel(h_ref, g_ref, w_ref, cw_ref, b_ref, o_ref, hn_ref, tail_ref, *,
                    tm, tiles_per_seq):
    i, j = pl.program_id(0), pl.program_id(1)

    @pl.when(j == 0)
    def _():
        hn_ref[...] = _rms(h_ref[...], g_ref[...]).astype(BF16)

    @pl.when((j == 0) & (i % tiles_per_seq == 0))
    def _():
        tail_ref[...] = jnp.zeros_like(tail_ref)

    acc = _dot(hn_ref[...], w_ref[...])
    ext = jnp.concatenate([tail_ref[j], acc], axis=0)
    out = acc * cw_ref[SSM_CONV - 1:SSM_CONV, :] + b_ref[...]
    for s in range(1, SSM_CONV):
        out = out + ext[8 - s:8 - s + tm, :] * cw_ref[SSM_CONV - 1 - s:SSM_CONV - s, :]
    o_ref[...] = _silu(out)
    tail_ref[j] = acc[tm - 8:tm, :]


def _conv_mm(h, g, w, layer, conv_w, conv_b, seq, *, col0, tm, tn):
    m, d = h.shape
    n = conv_w.shape[1]
    tm = min(tm, m, seq)
    assert seq % tm == 0 and n % tn == 0 and col0 % tn == 0
    return pl.pallas_call(
        functools.partial(_conv_mm_kernel, tm=tm, tiles_per_seq=seq // tm),
        grid=(m // tm, n // tn),
        in_specs=[
            pl.BlockSpec((tm, d), lambda i, j: (i, 0)),
            pl.BlockSpec((1, d), lambda i, j: (0, 0)),
            pl.BlockSpec((None, d, tn), lambda i, j: (layer, 0, col0 // tn + j)),
            pl.BlockSpec((SSM_CONV, tn), lambda i, j: (0, j)),
            pl.BlockSpec((1, tn), lambda i, j: (0, j)),
        ],
        out_specs=pl.BlockSpec((tm, tn), lambda i, j: (i, j)),
        out_shape=jax.ShapeDtypeStruct((m, n), F32),
        scratch_shapes=[pltpu.VMEM((tm, d), BF16), pltpu.VMEM((n // tn, 8, tn), F32)],
        compiler_params=_params("arbitrary", "arbitrary"),
        name="ssm_conv_mm",
    )(h, g.reshape(1, d), w, conv_w, conv_b.reshape(1, n))


def _ssd_kernel(z_ref, x_ref, b_ref, c_ref, dt_ref, alog_ref, d_ref, g_ref, tri_ref, exp_ref,
                o_ref, state_ref, *, q, groups):
    @pl.when(pl.program_id(2) == 0)
    def _():
        state_ref[...] = jnp.zeros_like(state_ref)

    gw, ns, gs = SSM_GROUP_WIDTH, SSM_STATE, range(groups)
    pairs = range(SSM_HEADS_PER_GROUP // 2)
    tri = tri_ref[...]
    expand2 = exp_ref[...]
    i_idx = lax.broadcasted_iota(jnp.int32, (q, q), 0)
    j_idx = lax.broadcasted_iota(jnp.int32, (q, q), 1)
    causal = j_idx <= i_idx
    left2 = lax.broadcasted_iota(jnp.int32, (2 * q, LANES), 1) < SSM_HEAD_DIM
    top2 = lax.broadcasted_iota(jnp.int32, (2 * q, LANES), 0) < q
    block_diag = left2 == top2

    def expand(v):
        hi, lo = _split2(v)
        return _dot(jnp.concatenate([hi, lo], axis=1), expand2)

    xs = [x_ref[:, g * gw:(g + 1) * gw] for g in gs]
    bm16 = [b_ref[:, g * ns:(g + 1) * ns].astype(BF16) for g in gs]
    cm16 = [c_ref[:, g * ns:(g + 1) * ns].astype(BF16) for g in gs]
    dts = [dt_ref[:, g * LANES:(g + 1) * LANES] for g in gs]
    das = [dts[g] * (-jnp.exp(alog_ref[:, g * LANES:(g + 1) * LANES])) for g in gs]
    splits = [_split3(da) for da in das]
    acums = [_dot(tri, hi) + _dot(tri, mid) + _dot(tri, lo) for hi, mid, lo in splits]
    states = [state_ref[g] for g in gs]
    cbs = [_dot_nt(cm16[g], bm16[g]) for g in gs]
    y_offs = [_dot(cm16[g], states[g].astype(BF16)) for g in gs]
    acum_ts = [acum.T for acum in acums]
    dt_ts = [dt.T for dt in dts]
    grows = [expand(jnp.exp(acum)) for acum in acums]
    tails = [expand(jnp.exp(acums[g][q - 1:q, :] - acums[g]) * dts[g]) for g in gs]

    y_diags = [[None] * len(pairs) for g in gs]
    for pair in pairs:
        sl = slice(pair * LANES, (pair + 1) * LANES)
        m_cats, x_bds = [], []
        for g in gs:
            mats = []
            for hh in (2 * pair, 2 * pair + 1):
                diff = acums[g][:, hh:hh + 1] - acum_ts[g][hh:hh + 1, :]
                decay = jnp.exp(jnp.where(causal, diff, NEG_BIG))
                mats.append((cbs[g] * decay * dt_ts[g][hh:hh + 1, :]).astype(BF16))
            m_cats.append(jnp.concatenate(mats, axis=1))
            xp = xs[g][:, sl]
            x_bds.append(jnp.where(block_diag, jnp.concatenate([xp, xp], axis=0), 0.0).astype(BF16))
        for g in gs:
            y_diags[g][pair] = _dot(m_cats[g], x_bds[g])

    xws = [(xs[g] * tails[g]).astype(BF16) for g in gs]
    updates = [_dot_tn(bm16[g], xws[g]) for g in gs]
    for g in gs:
        wide = slice(g * gw, (g + 1) * gw)
        state_ref[g] = states[g] * grows[g][q - 1:q, :] + updates[g]
        y = jnp.concatenate(y_diags[g], axis=1) + y_offs[g] * grows[g] + d_ref[:, wide] * xs[g]
        o_ref[:, wide] = _rms(y * z_ref[:, wide], g_ref[:, wide]).astype(o_ref.dtype)


def _ssd(z, xbc, dt, batch, seq, a_log, d_skip, g_out):
    q = min(SSD_CHUNK, seq)
    nc = seq // q
    gs = SSD_GROUPS_PER_STEP
    gw, ns, g = gs * SSM_GROUP_WIDTH, gs * SSM_STATE, SSM_GROUPS
    inner = SSM_GROUPS * SSM_GROUP_WIDTH
    b0 = inner // ns
    c0 = b0 + g // gs

    def row(b, gg, c):
        return b * nc + c

    tri = (lax.broadcasted_iota(jnp.int32, (q, q), 1)
           <= lax.broadcasted_iota(jnp.int32, (q, q), 0)).astype(BF16)
    expand = (lax.broadcasted_iota(jnp.int32, (LANES, SSM_GROUP_WIDTH), 0)
              == lax.broadcasted_iota(jnp.int32, (LANES, SSM_GROUP_WIDTH), 1) // SSM_HEAD_DIM).astype(BF16)
    expand2 = jnp.concatenate([expand, expand], axis=0)
    in_specs = [
        pl.BlockSpec((q, gw), lambda b, gg, c: (row(b, gg, c), gg)),
        pl.BlockSpec((q, gw), lambda b, gg, c: (row(b, gg, c), gg)),
        pl.BlockSpec((q, ns), lambda b, gg, c: (row(b, gg, c), b0 + gg)),
        pl.BlockSpec((q, ns), lambda b, gg, c: (row(b, gg, c), c0 + gg)),
        pl.BlockSpec((q, gs * LANES), lambda b, gg, c: (row(b, gg, c), gg)),
        pl.BlockSpec((1, gs * LANES), lambda b, gg, c: (0, gg)),
        pl.BlockSpec((1, gw), lambda b, gg, c: (0, gg)),
        pl.BlockSpec((1, gw), lambda b, gg, c: (0, gg)),
        pl.BlockSpec((q, q), lambda b, gg, c: (0, 0)),
        pl.BlockSpec((2 * LANES, SSM_GROUP_WIDTH), lambda b, gg, c: (0, 0)),
    ]
    return pl.pallas_call(
        functools.partial(_ssd_kernel, q=q, groups=gs),
        grid=(batch, g // gs, nc),
        in_specs=in_specs,
        out_specs=pl.BlockSpec((q, gw), lambda b, gg, c: (row(b, gg, c), gg)),
        out_shape=jax.ShapeDtypeStruct((batch * seq, inner), BF16),
        scratch_shapes=[pltpu.VMEM((gs, SSM_STATE, SSM_GROUP_WIDTH), F32)],
        compiler_params=_params("parallel", "parallel", "arbitrary"),
        name="ssd",
    )(z, xbc, xbc, xbc, dt, a_log, d_skip, g_out.reshape(1, -1), tri, expand2)


def _pad_heads(v):
    v = v.reshape(SSM_GROUPS, SSM_HEADS_PER_GROUP)
    v = jnp.pad(v, ((0, 0), (0, LANES - SSM_HEADS_PER_GROUP)))
    return v.reshape(1, SSM_GROUPS * LANES)


def _pad_dt_columns(w_in):
    lead = w_in.shape[:-1]
    heads = SSM_GROUPS * SSM_HEADS_PER_GROUP
    main, w_dt = w_in[..., :-heads], w_in[..., -heads:]
    w_dt = w_dt.reshape(*lead, SSM_GROUPS, SSM_HEADS_PER_GROUP)
    w_dt = jnp.pad(w_dt, [(0, 0)] * (len(lead) + 1) + [(0, LANES - SSM_HEADS_PER_GROUP)])
    return jnp.concatenate([main, w_dt.reshape(*lead, SSM_GROUPS * LANES)], axis=-1)


TILES_NORM_MM = dict(tm=1024, tn=1024)
TILES_RES_MM = dict(tm=1024, tn=1024, tk=2048)
TILES_PLE = dict(tm=1024, tn=512)


def kernel(x, p, attn_norm, attn_w_qkv, attn_q_norm, attn_k_norm, attn_w_o, ssm_norm, ssm_w_in, ssm_conv_w, ssm_conv_b, ssm_dt_bias, ssm_a_log, ssm_d, ssm_out_norm, ssm_w_out, mlp_norm, mlp_w_up, mlp_w_down, ple_w_proj, ple_norm, ple_gate_norm, ple_w_gate):
    batch, seq, d = x.shape
    m = batch * seq
    h = x.reshape(m, d)
    p = p.reshape(p.shape[0], m, PLE_DIM)
    w_qkv, w_o = attn_w_qkv.astype(BF16), attn_w_o.astype(BF16)
    w_in, w_out = _pad_dt_columns(ssm_w_in).astype(BF16), ssm_w_out.astype(BF16)
    w_up, w_down = mlp_w_up.astype(BF16), mlp_w_down.astype(BF16)
    w_gate, w_proj = ple_w_gate.astype(BF16), ple_w_proj.astype(BF16)
    inner = SSM_GROUPS * SSM_GROUP_WIDTH
    for i in range(DEPTH):
        j = i // 2
        if i % 2 == 0:
            qkv = _qkv_mm(h, attn_norm[j], w_qkv, j, attn_q_norm[j], attn_k_norm[j], **TILES_NORM_MM)
            o = _attention(qkv, batch, seq)
            h = _res_mm(o, w_o, j, h, **TILES_RES_MM)
        else:
            z = _norm_mm(h, ssm_norm[j], w_in, j, act="silu", out_dtype=F32, ncols=inner,
                         **TILES_NORM_MM)
            xbc = _conv_mm(h, ssm_norm[j], w_in, j, ssm_conv_w[j], ssm_conv_b[j], seq, col0=inner,
                           **TILES_NORM_MM)
            dt = _norm_mm(h, ssm_norm[j], w_in, j, act="softplus_bias", out_dtype=F32,
                          col0=inner + ssm_conv_w.shape[2], ncols=SSM_GROUPS * LANES,
                          bias=_pad_heads(ssm_dt_bias[j]), **TILES_NORM_MM)
            y = _ssd(z, xbc, dt, batch, seq, _pad_heads(ssm_a_log[j]),
                     jnp.repeat(ssm_d[j], SSM_HEAD_DIM).reshape(1, -1), ssm_out_norm[j])
            h = _res_mm(y, w_out, j, h, **TILES_RES_MM)
        u = _norm_mm(h, mlp_norm[i], w_up, i, act="relu2", out_dtype=BF16, **TILES_NORM_MM)
        h = _res_mm(u, w_down, i, h, **TILES_RES_MM)
        h = _ple(h, p, i, ple_gate_norm[i], w_gate, w_proj, ple_norm[i], **TILES_PLE)
    return h.reshape(batch, seq, d)
```

```python
import functools

import jax
import jax.numpy as jnp
from jax import lax
from jax.experimental import pallas as pl
from jax.experimental.pallas import tpu as pltpu

F32 = jnp.float32
BF16 = jnp.bfloat16

NORM_EPS = 1e-6
LANES = 128
VMEM_LIMIT = 48 * 1024 * 1024

DEPTH = 4
PLE_DIM = 256
SB_HEADS = 16
SB_HEAD_DIM = 128
SSM_HEAD_DIM = 64
SSM_GROUPS = 8
SSM_HEADS_PER_GROUP = 8
SSM_STATE = 128
SSM_CONV = 4
SSM_GROUP_WIDTH = SSM_HEADS_PER_GROUP * SSM_HEAD_DIM
SSD_CHUNK = 128
SSD_GROUPS_PER_STEP = 8
ATTN_BLOCK = 256
ATTN_HEADS_PER_STEP = 8
LOG2E = 1.4426950408889634
NEG_BIG = -1e30


def _params(*sem):
    return pltpu.CompilerParams(dimension_semantics=sem, vmem_limit_bytes=VMEM_LIMIT)


def _rms(x, g):
    ms = jnp.mean(x * x, axis=-1, keepdims=True)
    return x * lax.rsqrt(ms + NORM_EPS) * g


def _softplus(x):
    return jnp.maximum(x, 0.0) + jnp.log(1.0 + jnp.exp(-jnp.abs(x)))


def _softplus2(x):
    neg_abs = lax.bitcast_convert_type(
        lax.bitcast_convert_type(x, jnp.uint32) | jnp.uint32(0x80000000), F32)
    return jnp.maximum(x, 0.0) + jnp.log2(1.0 + jnp.exp2(neg_abs))


def _silu(x):
    return x * jax.nn.sigmoid(x)


def _split2(x):
    hi = x.astype(BF16)
    lo = (x - hi.astype(F32)).astype(BF16)
    return hi, lo


def _split3(x):
    hi = x.astype(BF16)
    r1 = x - hi.astype(F32)
    mid = r1.astype(BF16)
    lo = (r1 - mid.astype(F32)).astype(BF16)
    return hi, mid, lo


def _dot(a, b):
    return jnp.dot(a, b, preferred_element_type=F32)


def _dot_nt(a, b):
    return lax.dot_general(a, b, (((1,), (1,)), ((), ())), preferred_element_type=F32)


def _dot_tn(a, b):
    return lax.dot_general(a, b, (((0,), (0,)), ((), ())), preferred_element_type=F32)


def _norm_mm_kernel(h_ref, g_ref, w_ref, b_ref, o_ref, hn_ref, *, act):
    @pl.when(pl.program_id(1) == 0)
    def _():
        hn_ref[...] = _rms(h_ref[...], g_ref[...]).astype(BF16)

    acc = _dot(hn_ref[...], w_ref[...])
    if act == "relu2":
        r = jnp.maximum(acc, 0.0)
        acc = r * r
    elif act == "silu":
        acc = _silu(acc)
    elif act == "softplus_bias":
        acc = _softplus(acc + b_ref[...])
    o_ref[...] = acc.astype(o_ref.dtype)


def _norm_mm(h, g, w, layer, *, act, out_dtype, tm, tn, col0=0, ncols=None, bias=None):
    m, d = h.shape
    n = w.shape[2] if ncols is None else ncols
    tm, tn = min(tm, m), min(tn, n)
    assert col0 % tn == 0 and n % tn == 0
    if bias is None:
        bias = jnp.zeros((1, n), F32)
    return pl.pallas_call(
        functools.partial(_norm_mm_kernel, act=act),
        grid=(m // tm, n // tn),
        in_specs=[
            pl.BlockSpec((tm, d), lambda i, j: (i, 0)),
            pl.BlockSpec((1, d), lambda i, j: (0, 0)),
            pl.BlockSpec((None, d, tn), lambda i, j: (layer, 0, col0 // tn + j)),
            pl.BlockSpec((1, tn), lambda i, j: (0, j)),
        ],
        out_specs=pl.BlockSpec((tm, tn), lambda i, j: (i, j)),
        out_shape=jax.ShapeDtypeStruct((m, n), out_dtype),
        scratch_shapes=[pltpu.VMEM((tm, d), BF16)],
        compiler_params=_params("parallel", "arbitrary"),
        name="norm_mm_" + act,
    )(h, g.reshape(1, d), w, bias)


def _qkv_kernel(h_ref, g_ref, w_ref, gq_ref, gk_ref, o_ref, hn_ref, *, tn, q_tiles, k_tiles):
    j = pl.program_id(1)

    @pl.when(j == 0)
    def _():
        hn_ref[...] = _rms(h_ref[...], g_ref[...]).astype(BF16)

    @pl.when(j < k_tiles)
    def _():
        acc = _dot(hn_ref[...], w_ref[...])
        gh = jnp.where(j < q_tiles, gq_ref[...], gk_ref[...])
        for hh in range(tn // SB_HEAD_DIM):
            sl = slice(hh * SB_HEAD_DIM, (hh + 1) * SB_HEAD_DIM)
            o_ref[:, sl] = _rms(acc[:, sl], gh).astype(BF16)

    @pl.when(j >= k_tiles)
    def _():
        o_ref[...] = _dot(hn_ref[...], w_ref[...]).astype(BF16)


def _qkv_mm(h, g, w, layer, gq, gk, *, tm, tn):
    m, d = h.shape
    n = w.shape[2]
    width = n // 3
    tm = min(tm, m)
    assert width % tn == 0
    return pl.pallas_call(
        functools.partial(_qkv_kernel, tn=tn, q_tiles=width // tn, k_tiles=2 * width // tn),
        grid=(m // tm, n // tn),
        in_specs=[
            pl.BlockSpec((tm, d), lambda i, j: (i, 0)),
            pl.BlockSpec((1, d), lambda i, j: (0, 0)),
            pl.BlockSpec((None, d, tn), lambda i, j: (layer, 0, j)),
            pl.BlockSpec((1, SB_HEAD_DIM), lambda i, j: (0, 0)),
            pl.BlockSpec((1, SB_HEAD_DIM), lambda i, j: (0, 0)),
        ],
        out_specs=pl.BlockSpec((tm, tn), lambda i, j: (i, j)),
        out_shape=jax.ShapeDtypeStruct((m, n), BF16),
        scratch_shapes=[pltpu.VMEM((tm, d), BF16)],
        compiler_params=_params("parallel", "arbitrary"),
        name="qkv_mm",
    )(h, g.reshape(1, d), w, gq.reshape(1, -1), gk.reshape(1, -1))


def _res_mm_kernel(a_ref, w_ref, r_ref, o_ref, *, nk):
    if nk == 1:
        o_ref[...] = r_ref[...] + _dot(a_ref[...], w_ref[...])
        return
    k = pl.program_id(2)

    @pl.when(k == 0)
    def _():
        o_ref[...] = _dot(a_ref[...], w_ref[...])

    @pl.when((k > 0) & (k < nk - 1))
    def _():
        o_ref[...] += _dot(a_ref[...], w_ref[...])

    @pl.when(k == nk - 1)
    def _():
        o_ref[...] = r_ref[...] + (o_ref[...] + _dot(a_ref[...], w_ref[...]))


def _res_mm(a, w, layer, res, *, tm, tn, tk):
    m, kdim = a.shape
    n = w.shape[2]
    tm, tn, tk = min(tm, m), min(tn, n), min(tk, kdim)
    nk = kdim // tk
    return pl.pallas_call(
        functools.partial(_res_mm_kernel, nk=nk),
        grid=(m // tm, n // tn, nk),
        in_specs=[
            pl.BlockSpec((tm, tk), lambda i, j, k: (i, k)),
            pl.BlockSpec((None, tk, tn), lambda i, j, k: (layer, k, j)),
            pl.BlockSpec((tm, tn), lambda i, j, k: (i, j)),
        ],
        out_specs=pl.BlockSpec((tm, tn), lambda i, j, k: (i, j)),
        out_shape=jax.ShapeDtypeStruct((m, n), F32),
        compiler_params=_params("parallel", "parallel", "arbitrary"),
        name="res_mm",
    )(a, w, res)


def _ple_kernel(h_ref, p_ref, gg_ref, wg_ref, wp_ref, gp_ref, o_ref, hn_ref, e_ref, *, tn):
    j = pl.program_id(1)

    def gated(col):
        gate = jax.nn.sigmoid(_dot(hn_ref[...], wg_ref[...]))
        o_ref[...] = h_ref[:, pl.ds(col, tn)] + gate * e_ref[:, pl.ds(col, tn)]

    @pl.when(j == 0)
    def _():
        hn_ref[...] = _rms(h_ref[...], gg_ref[...]).astype(BF16)
        e_ref[...] = _rms(_dot(p_ref[...].astype(BF16), wp_ref[...]), gp_ref[...])
        gated(0)

    @pl.when(j > 0)
    def _():
        gated(pl.multiple_of(j * tn, tn))


def _ple(h, p, layer, gg, wg, wp, gp, *, tm, tn):
    m, d = h.shape
    pd = p.shape[2]
    tm, tn = min(tm, m), min(tn, d)
    return pl.pallas_call(
        functools.partial(_ple_kernel, tn=tn),
        grid=(m // tm, d // tn),
        in_specs=[
            pl.BlockSpec((tm, d), lambda i, j: (i, 0)),
            pl.BlockSpec((None, tm, pd), lambda i, j: (layer, i, 0)),
            pl.BlockSpec((1, d), lambda i, j: (0, 0)),
            pl.BlockSpec((None, d, tn), lambda i, j: (layer, 0, j)),
            pl.BlockSpec((None, pd, d), lambda i, j: (layer, 0, 0)),
            pl.BlockSpec((1, d), lambda i, j: (0, 0)),
        ],
        out_specs=pl.BlockSpec((tm, tn), lambda i, j: (i, j)),
        out_shape=jax.ShapeDtypeStruct((m, d), F32),
        scratch_shapes=[pltpu.VMEM((tm, d), BF16), pltpu.VMEM((tm, d), F32)],
        compiler_params=_params("parallel", "arbitrary"),
        name="ple",
    )(h, p, gg.reshape(1, d), wg, wp, gp.reshape(1, d))


def _attn_kernel(q_ref, k_ref, v_ref, t_ref, o_ref, *, blk, heads, scale):
    qi = pl.program_id(2)
    dh = SB_HEAD_DIM
    tri2 = t_ref[...]
    qs = [q_ref[:, hh * dh:(hh + 1) * dh] for hh in range(heads)]

    def tiles(kb, carries, masked):
        start = pl.multiple_of(kb * blk, blk)
        hs = range(heads)
        ks = [k_ref[pl.ds(start, blk), hh * dh:(hh + 1) * dh] for hh in hs]
        zs = [_dot_nt(qs[hh], ks[hh]) * (scale * LOG2E) for hh in hs]
        sps = [_softplus2(z) for z in zs]
        spms = sps
        if masked:
            t_idx = lax.broadcasted_iota(jnp.int32, (blk, blk), 0)
            s_idx = lax.broadcasted_iota(jnp.int32, (blk, blk), 1)
            mask = s_idx < t_idx
            spms = [jnp.where(mask, sp, 0.0) for sp in sps]
        splits = [_split2(spm) for spm in spms]
        sufs = [_dot(jnp.concatenate([hi, lo], axis=1), tri2) for hi, lo in splits]
        a_s = [jnp.exp2((zs[hh] - sps[hh]) - sufs[hh] + carries[hh]) for hh in hs]
        if masked:
            a_s = [jnp.where(mask, a, 0.0) for a in a_s]
        vs = [v_ref[pl.ds(start, blk), hh * dh:(hh + 1) * dh] for hh in hs]
        pvs = [_dot(a_s[hh].astype(BF16), vs[hh]) for hh in hs]
        new_carries = [carries[hh] - jnp.sum(spms[hh], axis=-1, keepdims=True) for hh in hs]
        return pvs, new_carries

    accs, carries = tiles(qi, [jnp.zeros((blk, 1), F32)] * heads, True)

    def body(it, state):
        accs, carries = state[:heads], state[heads:]
        pvs, carries = tiles(qi - 1 - it, carries, False)
        return tuple(acc + pv for acc, pv in zip(accs, pvs)) + tuple(carries)

    state = lax.fori_loop(0, qi, body, tuple(accs) + tuple(carries))
    for hh in range(heads):
        o_ref[:, hh * dh:(hh + 1) * dh] = state[hh].astype(o_ref.dtype)


def _attention(qkv, batch, seq):
    blk = min(ATTN_BLOCK, seq)
    nq = seq // blk
    hg = SB_HEADS // ATTN_HEADS_PER_STEP
    gw = ATTN_HEADS_PER_STEP * SB_HEAD_DIM
    tri = (lax.broadcasted_iota(jnp.int32, (blk, blk), 0)
           > lax.broadcasted_iota(jnp.int32, (blk, blk), 1)).astype(BF16)
    tri2 = jnp.concatenate([tri, tri], axis=0)
    return pl.pallas_call(
        functools.partial(_attn_kernel, blk=blk, heads=ATTN_HEADS_PER_STEP, scale=SB_HEAD_DIM ** -0.5),
        grid=(batch, hg, nq),
        in_specs=[
            pl.BlockSpec((blk, gw), lambda b, g, i: (b * nq + i, g)),
            pl.BlockSpec((seq, gw), lambda b, g, i: (b, hg + g)),
            pl.BlockSpec((seq, gw), lambda b, g, i: (b, 2 * hg + g)),
            pl.BlockSpec((2 * blk, blk), lambda b, g, i: (0, 0)),
        ],
        out_specs=pl.BlockSpec((blk, gw), lambda b, g, i: (b * nq + i, g)),
        out_shape=jax.ShapeDtypeStruct((batch * seq, SB_HEADS * SB_HEAD_DIM), BF16),
        compiler_params=_params("parallel", "parallel", "arbitrary"),
        name="stickbreak_attn",
    )(qkv, qkv, qkv, tri2)


def _conv_mm_kernel(h_ref, g_ref, w_ref, cw_ref, b_ref, o_ref, hn_ref, tail_ref, *,
                    tm, tiles_per_seq):
    i, j = pl.program_id(0), pl.program_id(1)

    @pl.when(j == 0)
    def _():
        hn_ref[...] = _rms(h_ref[...], g_ref[...]).astype(BF16)

    @pl.when((j == 0) & (i % tiles_per_seq == 0))
    def _():
        tail_ref[...] = jnp.zeros_like(tail_ref)

    acc = _dot(hn_ref[...], w_ref[...])
    ext = jnp.concatenate([tail_ref[j], acc], axis=0)
    out = acc * cw_ref[SSM_CONV - 1:SSM_CONV, :] + b_ref[...]
    for s in range(1, SSM_CONV):
        out = out + ext[8 - s:8 - s + tm, :] * cw_ref[SSM_CONV - 1 - s:SSM_CONV - s, :]
    o_ref[...] = _silu(out)
    tail_ref[j] = acc[tm - 8:tm, :]


def _conv_mm(h, g, w, layer, conv_w, conv_b, seq, *, col0, tm, tn):
    m, d = h.shape
    n = conv_w.shape[1]
    tm = min(tm, m, seq)
    assert seq % tm == 0 and n % tn == 0 and col0 % tn == 0
    return pl.pallas_call(
        functools.partial(_conv_mm_kernel, tm=tm, tiles_per_seq=seq // tm),
        grid=(m // tm, n // tn),
        in_specs=[
            pl.BlockSpec((tm, d), lambda i, j: (i, 0)),
            pl.BlockSpec((1, d), lambda i, j: (0, 0)),
            pl.BlockSpec((None, d, tn), lambda i, j: (layer, 0, col0 // tn + j)),
            pl.BlockSpec((SSM_CONV, tn), lambda i, j: (0, j)),
            pl.BlockSpec((1, tn), lambda i, j: (0, j)),
        ],
        out_specs=pl.BlockSpec((tm, tn), lambda i, j: (i, j)),
        out_shape=jax.ShapeDtypeStruct((m, n), F32),
        scratch_shapes=[pltpu.VMEM((tm, d), BF16), pltpu.VMEM((n // tn, 8, tn), F32)],
        compiler_params=_params("arbitrary", "arbitrary"),
        name="ssm_conv_mm",
    )(h, g.reshape(1, d), w, conv_w, conv_b.reshape(1, n))


def _ssd_kernel(z_ref, x_ref, b_ref, c_ref, dt_ref, alog_ref, d_ref, g_ref, tri_ref, exp_ref,
                o_ref, state_ref, *, q, groups):
    @pl.when(pl.program_id(2) == 0)
    def _():
        state_ref[...] = jnp.zeros_like(state_ref)

    gw, ns, gs = SSM_GROUP_WIDTH, SSM_STATE, range(groups)
    pairs = range(SSM_HEADS_PER_GROUP // 2)
    tri = tri_ref[...]
    expand2 = exp_ref[...]
    i_idx = lax.broadcasted_iota(jnp.int32, (q, q), 0)
    j_idx = lax.broadcasted_iota(jnp.int32, (q, q), 1)
    causal = j_idx <= i_idx
    left2 = lax.broadcasted_iota(jnp.int32, (2 * q, LANES), 1) < SSM_HEAD_DIM
    top2 = lax.broadcasted_iota(jnp.int32, (2 * q, LANES), 0) < q
    block_diag = left2 == top2

    def expand(v):
        hi, lo = _split2(v)
        return _dot(jnp.concatenate([hi, lo], axis=1), expand2)

    xs = [x_ref[:, g * gw:(g + 1) * gw] for g in gs]
    bm16 = [b_ref[:, g * ns:(g + 1) * ns].astype(BF16) for g in gs]
    cm16 = [c_ref[:, g * ns:(g + 1) * ns].astype(BF16) for g in gs]
    dts = [dt_ref[:, g * LANES:(g + 1) * LANES] for g in gs]
    das = [dts[g] * (-jnp.exp(alog_ref[:, g * LANES:(g + 1) * LANES])) for g in gs]
    splits = [_split3(da) for da in das]
    acums = [_dot(tri, hi) + _dot(tri, mid) + _dot(tri, lo) for hi, mid, lo in splits]
    states = [state_ref[g] for g in gs]
    cbs = [_dot_nt(cm16[g], bm16[g]) for g in gs]
    y_offs = [_dot(cm16[g], states[g].astype(BF16)) for g in gs]
    acum_ts = [acum.T for acum in acums]
    dt_ts = [dt.T for dt in dts]
    grows = [expand(jnp.exp(acum)) for acum in acums]
    tails = [expand(jnp.exp(acums[g][q - 1:q, :] - acums[g]) * dts[g]) for g in gs]

    y_diags = [[None] * len(pairs) for g in gs]
    for pair in pairs:
        sl = slice(pair * LANES, (pair + 1) * LANES)
        m_cats, x_bds = [], []
        for g in gs:
            mats = []
            for hh in (2 * pair, 2 * pair + 1):
                diff = acums[g][:, hh:hh + 1] - acum_ts[g][hh:hh + 1, :]
                decay = jnp.exp(jnp.where(causal, diff, NEG_BIG))
                mats.append((cbs[g] * decay * dt_ts[g][hh:hh + 1, :]).astype(BF16))
            m_cats.append(jnp.concatenate(mats, axis=1))
            xp = xs[g][:, sl]
            x_bds.append(jnp.where(block_diag, jnp.concatenate([xp, xp], axis=0), 0.0).astype(BF16))
        for g in gs:
            y_diags[g][pair] = _dot(m_cats[g], x_bds[g])

    xws = [(xs[g] * tails[g]).astype(BF16) for g in gs]
    updates = [_dot_tn(bm16[g], xws[g]) for g in gs]
    for g in gs:
        wide = slice(g * gw, (g + 1) * gw)
        state_ref[g] = states[g] * grows[g][q - 1:q, :] + updates[g]
        y = jnp.concatenate(y_diags[g], axis=1) + y_offs[g] * grows[g] + d_ref[:, wide] * xs[g]
        o_ref[:, wide] = _rms(y * z_ref[:, wide], g_ref[:, wide]).astype(o_ref.dtype)


def _ssd(z, xbc, dt, batch, seq, a_log, d_skip, g_out):
    q = min(SSD_CHUNK, seq)
    nc = seq // q
    gs = SSD_GROUPS_PER_STEP
    gw, ns, g = gs * SSM_GROUP_WIDTH, gs * SSM_STATE, SSM_GROUPS
    inner = SSM_GROUPS * SSM_GROUP_WIDTH
    b0 = inner // ns
    c0 = b0 + g // gs

    def row(b, gg, c):
        return b * nc + c

    tri = (lax.broadcasted_iota(jnp.int32, (q, q), 1)
           <= lax.broadcasted_iota(jnp.int32, (q, q), 0)).astype(BF16)
    expand = (lax.broadcasted_iota(jnp.int32, (LANES, SSM_GROUP_WIDTH), 0)
              == lax.broadcasted_iota(jnp.int32, (LANES, SSM_GROUP_WIDTH), 1) // SSM_HEAD_DIM).astype(BF16)
    expand2 = jnp.concatenate([expand, expand], axis=0)
    in_specs = [
        pl.BlockSpec((q, gw), lambda b, gg, c: (row(b, gg, c), gg)),
        pl.BlockSpec((q, gw), lambda b, gg, c: (row(b, gg, c), gg)),
        pl.BlockSpec((q, ns), lambda b, gg, c: (row(b, gg, c), b0 + gg)),
        pl.BlockSpec((q, ns), lambda b, gg, c: (row(b, gg, c), c0 + gg)),
        pl.BlockSpec((q, gs * LANES), lambda b, gg, c: (row(b, gg, c), gg)),
        pl.BlockSpec((1, gs * LANES), lambda b, gg, c: (0, gg)),
        pl.BlockSpec((1, gw), lambda b, gg, c: (0, gg)),
        pl.BlockSpec((1, gw), lambda b, gg, c: (0, gg)),
        pl.BlockSpec((q, q), lambda b, gg, c: (0, 0)),
        pl.BlockSpec((2 * LANES, SSM_GROUP_WIDTH), lambda b, gg, c: (0, 0)),
    ]
    return pl.pallas_call(
        functools.partial(_ssd_kernel, q=q, groups=gs),
        grid=(batch, g // gs, nc),
        in_specs=in_specs,
        out_specs=pl.BlockSpec((q, gw), lambda b, gg, c: (row(b, gg, c), gg)),
        out_shape=jax.ShapeDtypeStruct((batch * seq, inner), BF16),
        scratch_shapes=[pltpu.VMEM((gs, SSM_STATE, SSM_GROUP_WIDTH), F32)],
        compiler_params=_params("parallel", "parallel", "arbitrary"),
        name="ssd",
    )(z, xbc, xbc, xbc, dt, a_log, d_skip, g_out.reshape(1, -1), tri, expand2)


def _pad_heads(v):
    v = v.reshape(SSM_GROUPS, SSM_HEADS_PER_GROUP)
    v = jnp.pad(v, ((0, 0), (0, LANES - SSM_HEADS_PER_GROUP)))
    return v.reshape(1, SSM_GROUPS * LANES)


def _pad_dt_columns(w_in):
    lead = w_in.shape[:-1]
    heads = SSM_GROUPS * SSM_HEADS_PER_GROUP
    main, w_dt = w_in[..., :-heads], w_in[..., -heads:]
    w_dt = w_dt.reshape(*lead, SSM_GROUPS, SSM_HEADS_PER_GROUP)
    w_dt = jnp.pad(w_dt, [(0, 0)] * (len(lead) + 1) + [(0, LANES - SSM_HEADS_PER_GROUP)])
    return jnp.concatenate([main, w_dt.reshape(*lead, SSM_GROUPS * LANES)], axis=-1)


TILES_NORM_MM = dict(tm=1024, tn=1024)
TILES_RES_MM = dict(tm=1024, tn=1024, tk=2048)
TILES_PLE = dict(tm=1024, tn=512)


def kernel(x, p, attn_norm, attn_w_qkv, attn_q_norm, attn_k_norm, attn_w_o, ssm_norm, ssm_w_in, ssm_conv_w, ssm_conv_b, ssm_dt_bias, ssm_a_log, ssm_d, ssm_out_norm, ssm_w_out, mlp_norm, mlp_w_up, mlp_w_down, ple_w_proj, ple_norm, ple_gate_norm, ple_w_gate):
    batch, seq, d = x.shape
    m = batch * seq
    h = x.reshape(m, d)
    p = p.reshape(p.shape[0], m, PLE_DIM)
    w_qkv, w_o = attn_w_qkv.astype(BF16), attn_w_o.astype(BF16)
    w_in, w_out = _pad_dt_columns(ssm_w_in).astype(BF16), ssm_w_out.astype(BF16)
    w_up, w_down = mlp_w_up.astype(BF16), mlp_w_down.astype(BF16)
    w_gate, w_proj = ple_w_gate.astype(BF16), ple_w_proj.astype(BF16)
    inner = SSM_GROUPS * SSM_GROUP_WIDTH
    for i in range(DEPTH):
        j = i // 2
        if i % 2 == 0:
            qkv = _qkv_mm(h, attn_norm[j], w_qkv, j, attn_q_norm[j], attn_k_norm[j], **TILES_NORM_MM)
            o = _attention(qkv, batch, seq)
            h = _res_mm(o, w_o, j, h, **TILES_RES_MM)
        else:
            z = _norm_mm(h, ssm_norm[j], w_in, j, act="silu", out_dtype=F32, ncols=inner,
                         **TILES_NORM_MM)
            xbc = _conv_mm(h, ssm_norm[j], w_in, j, ssm_conv_w[j], ssm_conv_b[j], seq, col0=inner,
                           **TILES_NORM_MM)
            dt = _norm_mm(h, ssm_norm[j], w_in, j, act="softplus_bias", out_dtype=F32,
                          col0=inner + ssm_conv_w.shape[2], ncols=SSM_GROUPS * LANES,
                          bias=_pad_heads(ssm_dt_bias[j]), **TILES_NORM_MM)
            y = _ssd(z, xbc, dt, batch, seq, _pad_heads(ssm_a_log[j]),
                     jnp.repeat(ssm_d[j], SSM_HEAD_DIM).reshape(1, -1), ssm_out_norm[j])
            h = _res_mm(y, w_out, j, h, **TILES_RES_MM)
        u = _norm_mm(h, mlp_norm[i], w_up, i, act="relu2", out_dtype=BF16, **TILES_NORM_MM)
        h = _res_mm(u, w_down, i, h, **TILES_RES_MM)
        h = _ple(h, p, i, ple_gate_norm[i], w_gate, w_proj, ple_norm[i], **TILES_PLE)
    return h.reshape(batch, seq, d)
```

```python
import functools

import jax
import jax.numpy as jnp
from jax import lax
from jax.experimental import pallas as pl
from jax.experimental.pallas import tpu as pltpu

F32 = jnp.float32
BF16 = jnp.bfloat16

NORM_EPS = 1e-6
LANES = 128
VMEM_LIMIT = 48 * 1024 * 1024

DEPTH = 4
PLE_DIM = 256
SB_HEADS = 16
SB_HEAD_DIM = 128
SSM_HEAD_DIM = 64
SSM_GROUPS = 8
SSM_HEADS_PER_GROUP = 8
SSM_STATE = 128
SSM_CONV = 4
SSM_GROUP_WIDTH = SSM_HEADS_PER_GROUP * SSM_HEAD_DIM
SSD_CHUNK = 128
SSD_GROUPS_PER_STEP = 8
ATTN_BLOCK = 256
ATTN_HEADS_PER_STEP = 8
LOG2E = 1.4426950408889634
NEG_BIG = -1e30


def _params(*sem):
    return pltpu.CompilerParams(dimension_semantics=sem, vmem_limit_bytes=VMEM_LIMIT)


def _rms(x, g):
    ms = jnp.mean(x * x, axis=-1, keepdims=True)
    return x * lax.rsqrt(ms + NORM_EPS) * g


def _softplus(x):
    return jnp.maximum(x, 0.0) + jnp.log(1.0 + jnp.exp(-jnp.abs(x)))


def _softplus2(x):
    neg_abs = lax.bitcast_convert_type(
        lax.bitcast_convert_type(x, jnp.uint32) | jnp.uint32(0x80000000), F32)
    return jnp.maximum(x, 0.0) + jnp.log2(1.0 + jnp.exp2(neg_abs))


def _silu(x):
    return x * jax.nn.sigmoid(x)


def _split2(x):
    hi = x.astype(BF16)
    lo = (x - hi.astype(F32)).astype(BF16)
    return hi, lo


def _split3(x):
    hi = x.astype(BF16)
    r1 = x - hi.astype(F32)
    mid = r1.astype(BF16)
    lo = (r1 - mid.astype(F32)).astype(BF16)
    return hi, mid, lo


def _dot(a, b):
    return jnp.dot(a, b, preferred_element_type=F32)


def _dot_nt(a, b):
    return lax.dot_general(a, b, (((1,), (1,)), ((), ())), preferred_element_type=F32)


def _dot_tn(a, b):
    return lax.dot_general(a, b, (((0,), (0,)), ((), ())), preferred_element_type=F32)


def _norm_mm_kernel(h_ref, g_ref, w_ref, o_ref, hn_ref):
    @pl.when(pl.program_id(1) == 0)
    def _():
        hn_ref[...] = _rms(h_ref[...], g_ref[...]).astype(BF16)

    r = jnp.maximum(_dot(hn_ref[...], w_ref[...]), 0.0)
    o_ref[...] = (r * r).astype(o_ref.dtype)


def _norm_mm(h, g, w, layer, *, out_dtype, tm, tn):
    m, d = h.shape
    n = w.shape[2]
    tm, tn = min(tm, m), min(tn, n)
    return pl.pallas_call(
        _norm_mm_kernel,
        grid=(m // tm, n // tn),
        in_specs=[
            pl.BlockSpec((tm, d), lambda i, j: (i, 0)),
            pl.BlockSpec((1, d), lambda i, j: (0, 0)),
            pl.BlockSpec((None, d, tn), lambda i, j: (layer, 0, j)),
        ],
        out_specs=pl.BlockSpec((tm, tn), lambda i, j: (i, j)),
        out_shape=jax.ShapeDtypeStruct((m, n), out_dtype),
        scratch_shapes=[pltpu.VMEM((tm, d), BF16)],
        compiler_params=_params("parallel", "arbitrary"),
        name="norm_mm_relu2",
    )(h, g.reshape(1, d), w)


def _qkv_kernel(h_ref, g_ref, w_ref, gq_ref, gk_ref, o_ref, hn_ref, *, tn, q_tiles, k_tiles):
    j = pl.program_id(1)

    @pl.when(j == 0)
    def _():
        hn_ref[...] = _rms(h_ref[...], g_ref[...]).astype(BF16)

    @pl.when(j < k_tiles)
    def _():
        acc = _dot(hn_ref[...], w_ref[...])
        gh = jnp.where(j < q_tiles, gq_ref[...], gk_ref[...])
        for hh in range(tn // SB_HEAD_DIM):
            sl = slice(hh * SB_HEAD_DIM, (hh + 1) * SB_HEAD_DIM)
            o_ref[:, sl] = _rms(acc[:, sl], gh).astype(BF16)

    @pl.when(j >= k_tiles)
    def _():
        o_ref[...] = _dot(hn_ref[...], w_ref[...]).astype(BF16)


def _qkv_mm(h, g, w, layer, gq, gk, *, tm, tn):
    m, d = h.shape
    n = w.shape[2]
    width = n // 3
    tm = min(tm, m)
    assert width % tn == 0
    return pl.pallas_call(
        functools.partial(_qkv_kernel, tn=tn, q_tiles=width // tn, k_tiles=2 * width // tn),
        grid=(m // tm, n // tn),
        in_specs=[
            pl.BlockSpec((tm, d), lambda i, j: (i, 0)),
            pl.BlockSpec((1, d), lambda i, j: (0, 0)),
            pl.BlockSpec((None, d, tn), lambda i, j: (layer, 0, j)),
            pl.BlockSpec((1, SB_HEAD_DIM), lambda i, j: (0, 0)),
            pl.BlockSpec((1, SB_HEAD_DIM), lambda i, j: (0, 0)),
        ],
        out_specs=pl.BlockSpec((tm, tn), lambda i, j: (i, j)),
        out_shape=jax.ShapeDtypeStruct((m, n), BF16),
        scratch_shapes=[pltpu.VMEM((tm, d), BF16)],
        compiler_params=_params("parallel", "arbitrary"),
        name="qkv_mm",
    )(h, g.reshape(1, d), w, gq.reshape(1, -1), gk.reshape(1, -1))


def _res_mm_kernel(a_ref, w_ref, r_ref, o_ref, *, nk):
    if nk == 1:
        o_ref[...] = r_ref[...] + _dot(a_ref[...], w_ref[...])
        return
    k = pl.program_id(2)

    @pl.when(k == 0)
    def _():
        o_ref[...] = _dot(a_ref[...], w_ref[...])

    @pl.when((k > 0) & (k < nk - 1))
    def _():
        o_ref[...] += _dot(a_ref[...], w_ref[...])

    @pl.when(k == nk - 1)
    def _():
        o_ref[...] = r_ref[...] + (o_ref[...] + _dot(a_ref[...], w_ref[...]))


def _res_mm(a, w, layer, res, *, tm, tn, tk):
    m, kdim = a.shape
    n = w.shape[2]
    tm, tn, tk = min(tm, m), min(tn, n), min(tk, kdim)
    nk = kdim // tk
    return pl.pallas_call(
        functools.partial(_res_mm_kernel, nk=nk),
        grid=(m // tm, n // tn, nk),
        in_specs=[
            pl.BlockSpec((tm, tk), lambda i, j, k: (i, k)),
            pl.BlockSpec((None, tk, tn), lambda i, j, k: (layer, k, j)),
            pl.BlockSpec((tm, tn), lambda i, j, k: (i, j)),
        ],
        out_specs=pl.BlockSpec((tm, tn), lambda i, j, k: (i, j)),
        out_shape=jax.ShapeDtypeStruct((m, n), F32),
        compiler_params=_params("parallel", "parallel", "arbitrary"),
        name="res_mm",
    )(a, w, res)


def _ple_kernel(h_ref, p_ref, gg_ref, wg_ref, wp_ref, gp_ref, o_ref, hn_ref, e_ref, *, tn):
    j = pl.program_id(1)

    def gated(col):
        gate = jax.nn.sigmoid(_dot(hn_ref[...], wg_ref[...]))
        o_ref[...] = h_ref[:, pl.ds(col, tn)] + gate * e_ref[:, pl.ds(col, tn)]

    @pl.when(j == 0)
    def _():
        hn_ref[...] = _rms(h_ref[...], gg_ref[...]).astype(BF16)
        e_ref[...] = _rms(_dot(p_ref[...].astype(BF16), wp_ref[...]), gp_ref[...])
        gated(0)

    @pl.when(j > 0)
    def _():
        gated(pl.multiple_of(j * tn, tn))


def _ple(h, p, layer, gg, wg, wp, gp, *, tm, tn):
    m, d = h.shape
    pd = p.shape[2]
    tm, tn = min(tm, m), min(tn, d)
    return pl.pallas_call(
        functools.partial(_ple_kernel, tn=tn),
        grid=(m // tm, d // tn),
        in_specs=[
            pl.BlockSpec((tm, d), lambda i, j: (i, 0)),
            pl.BlockSpec((None, tm, pd), lambda i, j: (layer, i, 0)),
            pl.BlockSpec((1, d), lambda i, j: (0, 0)),
            pl.BlockSpec((None, d, tn), lambda i, j: (layer, 0, j)),
            pl.BlockSpec((None, pd, d), lambda i, j: (layer, 0, 0)),
            pl.BlockSpec((1, d), lambda i, j: (0, 0)),
        ],
        out_specs=pl.BlockSpec((tm, tn), lambda i, j: (i, j)),
        out_shape=jax.ShapeDtypeStruct((m, d), F32),
        scratch_shapes=[pltpu.VMEM((tm, d), BF16), pltpu.VMEM((tm, d), F32)],
        compiler_params=_params("parallel", "arbitrary"),
        name="ple",
    )(h, p, gg.reshape(1, d), wg, wp, gp.reshape(1, d))


def _attn_kernel(q_ref, k_ref, v_ref, t_ref, o_ref, *, blk, heads, scale):
    qi = pl.program_id(2)
    dh = SB_HEAD_DIM
    tri2 = t_ref[...]
    qs = [q_ref[:, hh * dh:(hh + 1) * dh] for hh in range(heads)]

    def tiles(kb, carries, masked):
        start = pl.multiple_of(kb * blk, blk)
        hs = range(heads)
        ks = [k_ref[pl.ds(start, blk), hh * dh:(hh + 1) * dh] for hh in hs]
        zs = [_dot_nt(qs[hh], ks[hh]) * (scale * LOG2E) for hh in hs]
        sps = [_softplus2(z) for z in zs]
        spms = sps
        if masked:
            t_idx = lax.broadcasted_iota(jnp.int32, (blk, blk), 0)
            s_idx = lax.broadcasted_iota(jnp.int32, (blk, blk), 1)
            mask = s_idx < t_idx
            spms = [jnp.where(mask, sp, 0.0) for sp in sps]
        splits = [_split2(spm) for spm in spms]
        sufs = [_dot(jnp.concatenate([hi, lo], axis=1), tri2) for hi, lo in splits]
        a_s = [jnp.exp2((zs[hh] - sps[hh]) - sufs[hh] + carries[hh]) for hh in hs]
        if masked:
            a_s = [jnp.where(mask, a, 0.0) for a in a_s]
        vs = [v_ref[pl.ds(start, blk), hh * dh:(hh + 1) * dh] for hh in hs]
        pvs = [_dot(a_s[hh].astype(BF16), vs[hh]) for hh in hs]
        new_carries = [carries[hh] - jnp.sum(spms[hh], axis=-1, keepdims=True) for hh in hs]
        return pvs, new_carries

    accs, carries = tiles(qi, [jnp.zeros((blk, 1), F32)] * heads, True)

    def body(it, state):
        accs, carries = state[:heads], state[heads:]
        pvs, carries = tiles(qi - 1 - it, carries, False)
        return tuple(acc + pv for acc, pv in zip(accs, pvs)) + tuple(carries)

    state = lax.fori_loop(0, qi, body, tuple(accs) + tuple(carries))
    for hh in range(heads):
        o_ref[:, hh * dh:(hh + 1) * dh] = state[hh].astype(o_ref.dtype)


def _attention(qkv, batch, seq):
    blk = min(ATTN_BLOCK, seq)
    nq = seq // blk
    hg = SB_HEADS // ATTN_HEADS_PER_STEP
    gw = ATTN_HEADS_PER_STEP * SB_HEAD_DIM
    tri = (lax.broadcasted_iota(jnp.int32, (blk, blk), 0)
           > lax.broadcasted_iota(jnp.int32, (blk, blk), 1)).astype(BF16)
    tri2 = jnp.concatenate([tri, tri], axis=0)
    return pl.pallas_call(
        functools.partial(_attn_kernel, blk=blk, heads=ATTN_HEADS_PER_STEP, scale=SB_HEAD_DIM ** -0.5),
        grid=(batch, hg, nq),
        in_specs=[
            pl.BlockSpec((blk, gw), lambda b, g, i: (b * nq + i, g)),
            pl.BlockSpec((seq, gw), lambda b, g, i: (b, hg + g)),
            pl.BlockSpec((seq, gw), lambda b, g, i: (b, 2 * hg + g)),
            pl.BlockSpec((2 * blk, blk), lambda b, g, i: (0, 0)),
        ],
        out_specs=pl.BlockSpec((blk, gw), lambda b, g, i: (b * nq + i, g)),
        out_shape=jax.ShapeDtypeStruct((batch * seq, SB_HEADS * SB_HEAD_DIM), BF16),
        compiler_params=_params("parallel", "parallel", "arbitrary"),
        name="stickbreak_attn",
    )(qkv, qkv, qkv, tri2)


def _inproj_kernel(h_ref, g_ref, w_ref, cw_ref, b_ref, o_ref, hn_ref, tail_ref, *,
                   tm, nj, z_tiles, tiles_per_seq):
    i, j = pl.program_id(0), pl.program_id(1)

    @pl.when(j == 0)
    def _():
        hn_ref[...] = _rms(h_ref[...], g_ref[...]).astype(BF16)

    @pl.when((j == 0) & (i % tiles_per_seq == 0))
    def _():
        tail_ref[...] = jnp.zeros_like(tail_ref)

    @pl.when(j < z_tiles)
    def _():
        o_ref[...] = _silu(_dot(hn_ref[...], w_ref[...]))

    @pl.when((j >= z_tiles) & (j < nj - 1))
    def _():
        acc = _dot(hn_ref[...], w_ref[...])
        ext = jnp.concatenate([tail_ref[j - z_tiles], acc], axis=0)
        out = acc * cw_ref[SSM_CONV - 1:SSM_CONV, :] + b_ref[...]
        for s in range(1, SSM_CONV):
            out = out + ext[8 - s:8 - s + tm, :] * cw_ref[SSM_CONV - 1 - s:SSM_CONV - s, :]
        o_ref[...] = _silu(out)
        tail_ref[j - z_tiles] = acc[tm - 8:tm, :]

    @pl.when(j == nj - 1)
    def _():
        o_ref[...] = _softplus(_dot(hn_ref[...], w_ref[...]) + b_ref[...])


def _inproj(h, g, w, layer, conv_w, bias, seq, *, tm, tn):
    m, d = h.shape
    n = w.shape[2]
    tm = min(tm, m, seq)
    inner = SSM_GROUPS * SSM_GROUP_WIDTH
    ni, nj = m // tm, n // tn
    assert seq % tm == 0 and inner % tn == 0 and n - tn == 2 * inner + 2 * SSM_GROUPS * SSM_STATE
    z_tiles = inner // tn
    kern = functools.partial(_inproj_kernel, tm=tm, nj=nj, z_tiles=z_tiles, tiles_per_seq=seq // tm)
    return pl.pallas_call(
        kern,
        grid=(ni, nj),
        in_specs=[
            pl.BlockSpec((tm, d), lambda i, j: (i, 0)),
            pl.BlockSpec((1, d), lambda i, j: (0, 0)),
            pl.BlockSpec((None, d, tn), lambda i, j: (layer, 0, j)),
            pl.BlockSpec((SSM_CONV, tn), lambda i, j: (0, j)),
            pl.BlockSpec((1, tn), lambda i, j: (0, j)),
        ],
        out_specs=pl.BlockSpec((tm, tn), lambda i, j: (i, j)),
        out_shape=jax.ShapeDtypeStruct((m, n), F32),
        scratch_shapes=[pltpu.VMEM((tm, d), BF16),
                        pltpu.VMEM((nj - 1 - z_tiles, 8, tn), F32)],
        compiler_params=_params("arbitrary", "arbitrary"),
        name="ssm_inproj",
    )(h, g.reshape(1, d), w, conv_w, bias)


def _ssd_kernel(z_ref, x_ref, b_ref, c_ref, dt_ref, alog_ref, d_ref, g_ref, tri_ref, exp_ref,
                o_ref, state_ref, *, q, groups):
    @pl.when(pl.program_id(2) == 0)
    def _():
        state_ref[...] = jnp.zeros_like(state_ref)

    gw, ns, gs = SSM_GROUP_WIDTH, SSM_STATE, range(groups)
    pairs = range(SSM_HEADS_PER_GROUP // 2)
    tri = tri_ref[...]
    expand2 = exp_ref[...]
    i_idx = lax.broadcasted_iota(jnp.int32, (q, q), 0)
    j_idx = lax.broadcasted_iota(jnp.int32, (q, q), 1)
    causal = j_idx <= i_idx
    left2 = lax.broadcasted_iota(jnp.int32, (2 * q, LANES), 1) < SSM_HEAD_DIM
    top2 = lax.broadcasted_iota(jnp.int32, (2 * q, LANES), 0) < q
    block_diag = left2 == top2

    def expand(v):
        hi, lo = _split2(v)
        return _dot(jnp.concatenate([hi, lo], axis=1), expand2)

    xs = [x_ref[:, g * gw:(g + 1) * gw] for g in gs]
    bm16 = [b_ref[:, g * ns:(g + 1) * ns].astype(BF16) for g in gs]
    cm16 = [c_ref[:, g * ns:(g + 1) * ns].astype(BF16) for g in gs]
    dts = [dt_ref[:, g * LANES:(g + 1) * LANES] for g in gs]
    das = [dts[g] * (-jnp.exp(alog_ref[:, g * LANES:(g + 1) * LANES])) for g in gs]
    splits = [_split3(da) for da in das]
    acums = [_dot(tri, hi) + _dot(tri, mid) + _dot(tri, lo) for hi, mid, lo in splits]
    states = [state_ref[g] for g in gs]
    cbs = [_dot_nt(cm16[g], bm16[g]) for g in gs]
    y_offs = [_dot(cm16[g], states[g].astype(BF16)) for g in gs]
    acum_ts = [acum.T for acum in acums]
    dt_ts = [dt.T for dt in dts]
    grows = [expand(jnp.exp(acum)) for acum in acums]
    tails = [expand(jnp.exp(acums[g][q - 1:q, :] - acums[g]) * dts[g]) for g in gs]

    y_diags = [[None] * len(pairs) for g in gs]
    for pair in pairs:
        sl = slice(pair * LANES, (pair + 1) * LANES)
        m_cats, x_bds = [], []
        for g in gs:
            mats = []
            for hh in (2 * pair, 2 * pair + 1):
                diff = acums[g][:, hh:hh + 1] - acum_ts[g][hh:hh + 1, :]
                decay = jnp.exp(jnp.where(causal, diff, NEG_BIG))
                mats.append((cbs[g] * decay * dt_ts[g][hh:hh + 1, :]).astype(BF16))
            m_cats.append(jnp.concatenate(mats, axis=1))
            xp = xs[g][:, sl]
            x_bds.append(jnp.where(block_diag, jnp.concatenate([xp, xp], axis=0), 0.0).astype(BF16))
        for g in gs:
            y_diags[g][pair] = _dot(m_cats[g], x_bds[g])

    xws = [(xs[g] * tails[g]).astype(BF16) for g in gs]
    updates = [_dot_tn(bm16[g], xws[g]) for g in gs]
    for g in gs:
        wide = slice(g * gw, (g + 1) * gw)
        state_ref[g] = states[g] * grows[g][q - 1:q, :] + updates[g]
        y = jnp.concatenate(y_diags[g], axis=1) + y_offs[g] * grows[g] + d_ref[:, wide] * xs[g]
        o_ref[:, wide] = _rms(y * z_ref[:, wide], g_ref[:, wide]).astype(o_ref.dtype)


def _ssd(zx, batch, seq, a_log, d_skip, g_out):
    q = min(SSD_CHUNK, seq)
    nc = seq // q
    gs = SSD_GROUPS_PER_STEP
    gw, ns, g = gs * SSM_GROUP_WIDTH, gs * SSM_STATE, SSM_GROUPS
    inner = SSM_GROUPS * SSM_GROUP_WIDTH
    x0, b0 = inner // gw, 2 * inner // ns
    c0, dt0 = b0 + g // gs, b0 + 2 * (g // gs)

    def row(b, gg, c):
        return b * nc + c

    tri = (lax.broadcasted_iota(jnp.int32, (q, q), 1)
           <= lax.broadcasted_iota(jnp.int32, (q, q), 0)).astype(BF16)
    expand = (lax.broadcasted_iota(jnp.int32, (LANES, SSM_GROUP_WIDTH), 0)
              == lax.broadcasted_iota(jnp.int32, (LANES, SSM_GROUP_WIDTH), 1) // SSM_HEAD_DIM).astype(BF16)
    expand2 = jnp.concatenate([expand, expand], axis=0)
    in_specs = [
        pl.BlockSpec((q, gw), lambda b, gg, c: (row(b, gg, c), gg)),
        pl.BlockSpec((q, gw), lambda b, gg, c: (row(b, gg, c), x0 + gg)),
        pl.BlockSpec((q, ns), lambda b, gg, c: (row(b, gg, c), b0 + gg)),
        pl.BlockSpec((q, ns), lambda b, gg, c: (row(b, gg, c), c0 + gg)),
        pl.BlockSpec((q, gs * LANES), lambda b, gg, c: (row(b, gg, c), dt0 + gg)),
        pl.BlockSpec((1, gs * LANES), lambda b, gg, c: (0, gg)),
        pl.BlockSpec((1, gw), lambda b, gg, c: (0, gg)),
        pl.BlockSpec((1, gw), lambda b, gg, c: (0, gg)),
        pl.BlockSpec((q, q), lambda b, gg, c: (0, 0)),
        pl.BlockSpec((2 * LANES, SSM_GROUP_WIDTH), lambda b, gg, c: (0, 0)),
    ]
    return pl.pallas_call(
        functools.partial(_ssd_kernel, q=q, groups=gs),
        grid=(batch, g // gs, nc),
        in_specs=in_specs,
        out_specs=pl.BlockSpec((q, gw), lambda b, gg, c: (row(b, gg, c), gg)),
        out_shape=jax.ShapeDtypeStruct((batch * seq, inner), BF16),
        scratch_shapes=[pltpu.VMEM((gs, SSM_STATE, SSM_GROUP_WIDTH), F32)],
        compiler_params=_params("parallel", "parallel", "arbitrary"),
        name="ssd",
    )(zx, zx, zx, zx, zx, a_log, d_skip, g_out.reshape(1, -1), tri, expand2)


def _pad_heads(v):
    v = v.reshape(SSM_GROUPS, SSM_HEADS_PER_GROUP)
    v = jnp.pad(v, ((0, 0), (0, LANES - SSM_HEADS_PER_GROUP)))
    return v.reshape(1, SSM_GROUPS * LANES)


def _pad_dt_columns(w_in):
    lead = w_in.shape[:-1]
    heads = SSM_GROUPS * SSM_HEADS_PER_GROUP
    main, w_dt = w_in[..., :-heads], w_in[..., -heads:]
    w_dt = w_dt.reshape(*lead, SSM_GROUPS, SSM_HEADS_PER_GROUP)
    w_dt = jnp.pad(w_dt, [(0, 0)] * (len(lead) + 1) + [(0, LANES - SSM_HEADS_PER_GROUP)])
    return jnp.concatenate([main, w_dt.reshape(*lead, SSM_GROUPS * LANES)], axis=-1)


TILES_NORM_MM = dict(tm=1024, tn=1024)
TILES_MLP_UP = dict(tm=1024, tn=2048)
TILES_RES_MM = dict(tm=1024, tn=1024, tk=2048)
TILES_PLE = dict(tm=1024, tn=512)


def kernel(x, p, attn_norm, attn_w_qkv, attn_q_norm, attn_k_norm, attn_w_o, ssm_norm, ssm_w_in, ssm_conv_w, ssm_conv_b, ssm_dt_bias, ssm_a_log, ssm_d, ssm_out_norm, ssm_w_out, mlp_norm, mlp_w_up, mlp_w_down, ple_w_proj, ple_norm, ple_gate_norm, ple_w_gate):
    batch, seq, d = x.shape
    m = batch * seq
    h = x.reshape(m, d)
    p = p.reshape(p.shape[0], m, PLE_DIM)
    w_qkv, w_o = attn_w_qkv.astype(BF16), attn_w_o.astype(BF16)
    w_in, w_out = _pad_dt_columns(ssm_w_in).astype(BF16), ssm_w_out.astype(BF16)
    w_up, w_down = mlp_w_up.astype(BF16), mlp_w_down.astype(BF16)
    w_gate, w_proj = ple_w_gate.astype(BF16), ple_w_proj.astype(BF16)
    inner = SSM_GROUPS * SSM_GROUP_WIDTH
    for i in range(DEPTH):
        j = i // 2
        if i % 2 == 0:
            qkv = _qkv_mm(h, attn_norm[j], w_qkv, j, attn_q_norm[j], attn_k_norm[j], **TILES_NORM_MM)
            o = _attention(qkv, batch, seq)
            h = _res_mm(o, w_o, j, h, **TILES_RES_MM)
        else:
            conv_w = jnp.pad(ssm_conv_w[j], ((0, 0), (inner, SSM_GROUPS * LANES)))
            bias = jnp.concatenate([jnp.zeros((1, inner), F32), ssm_conv_b[j].reshape(1, -1),
                                    _pad_heads(ssm_dt_bias[j])], axis=1)
            zx = _inproj(h, ssm_norm[j], w_in, j, conv_w, bias, seq, **TILES_NORM_MM)
            y = _ssd(zx, batch, seq, _pad_heads(ssm_a_log[j]),
                     jnp.repeat(ssm_d[j], SSM_HEAD_DIM).reshape(1, -1), ssm_out_norm[j])
            h = _res_mm(y, w_out, j, h, **TILES_RES_MM)
        u = _norm_mm(h, mlp_norm[i], w_up, i, out_dtype=BF16, **TILES_MLP_UP)
        h = _res_mm(u, w_down, i, h, **TILES_RES_MM)
        h = _ple(h, p, i, ple_gate_norm[i], w_gate, w_proj, ple_norm[i], **TILES_PLE)
    return h.reshape(batch, seq, d)
```

```python
import functools

import jax
import jax.numpy as jnp
from jax import lax
from jax.experimental import pallas as pl
from jax.experimental.pallas import tpu as pltpu

F32 = jnp.float32
BF16 = jnp.bfloat16

NORM_EPS = 1e-6
LANES = 128
VMEM_LIMIT = 56 * 1024 * 1024

DEPTH = 4
PLE_DIM = 256
SB_HEADS = 16
SB_HEAD_DIM = 128
SSM_HEAD_DIM = 64
SSM_GROUPS = 8
SSM_HEADS_PER_GROUP = 8
SSM_STATE = 128
SSM_CONV = 4
SSM_GROUP_WIDTH = SSM_HEADS_PER_GROUP * SSM_HEAD_DIM
SSD_CHUNK = 128
SSD_GROUPS_PER_STEP = 8
ATTN_BLOCK = 256
ATTN_HEADS_PER_STEP = 8
LOG2E = 1.4426950408889634
NEG_BIG = -1e30


def _params(*sem):
    return pltpu.CompilerParams(dimension_semantics=sem, vmem_limit_bytes=VMEM_LIMIT)


def _rms(x, g):
    ms = jnp.mean(x * x, axis=-1, keepdims=True)
    return x * lax.rsqrt(ms + NORM_EPS) * g


def _softplus(x):
    return jnp.maximum(x, 0.0) + jnp.log(1.0 + jnp.exp(-jnp.abs(x)))


def _softplus2(x):
    neg_abs = lax.bitcast_convert_type(
        lax.bitcast_convert_type(x, jnp.uint32) | jnp.uint32(0x80000000), F32)
    return jnp.maximum(x, 0.0) + jnp.log2(1.0 + jnp.exp2(neg_abs))


def _silu(x):
    return x * jax.nn.sigmoid(x)


def _split2(x):
    hi = x.astype(BF16)
    lo = (x - hi.astype(F32)).astype(BF16)
    return hi, lo


def _split3(x):
    hi = x.astype(BF16)
    r1 = x - hi.astype(F32)
    mid = r1.astype(BF16)
    lo = (r1 - mid.astype(F32)).astype(BF16)
    return hi, mid, lo


def _dot(a, b):
    return jnp.dot(a, b, preferred_element_type=F32)


def _dot_nt(a, b):
    return lax.dot_general(a, b, (((1,), (1,)), ((), ())), preferred_element_type=F32)


def _dot_tn(a, b):
    return lax.dot_general(a, b, (((0,), (0,)), ((), ())), preferred_element_type=F32)


def _norm_mm_kernel(h_ref, g_ref, w_ref, o_ref, hn_ref):
    @pl.when(pl.program_id(1) == 0)
    def _():
        hn_ref[...] = _rms(h_ref[...], g_ref[...]).astype(BF16)

    r = jnp.maximum(_dot(hn_ref[...], w_ref[...]), 0.0)
    o_ref[...] = (r * r).astype(o_ref.dtype)


def _norm_mm(h, g, w, layer, *, out_dtype, tm, tn):
    m, d = h.shape
    n = w.shape[2]
    tm, tn = min(tm, m), min(tn, n)
    return pl.pallas_call(
        _norm_mm_kernel,
        grid=(m // tm, n // tn),
        in_specs=[
            pl.BlockSpec((tm, d), lambda i, j: (i, 0)),
            pl.BlockSpec((1, d), lambda i, j: (0, 0)),
            pl.BlockSpec((None, d, tn), lambda i, j: (layer, 0, j)),
        ],
        out_specs=pl.BlockSpec((tm, tn), lambda i, j: (i, j)),
        out_shape=jax.ShapeDtypeStruct((m, n), out_dtype),
        scratch_shapes=[pltpu.VMEM((tm, d), BF16)],
        compiler_params=_params("parallel", "arbitrary"),
        name="norm_mm_relu2",
    )(h, g.reshape(1, d), w)


def _qkv_kernel(h_ref, g_ref, w_ref, gq_ref, gk_ref, o_ref, hn_ref, *, tn, q_tiles, k_tiles):
    j = pl.program_id(1)

    @pl.when(j == 0)
    def _():
        hn_ref[...] = _rms(h_ref[...], g_ref[...]).astype(BF16)

    @pl.when(j < k_tiles)
    def _():
        acc = _dot(hn_ref[...], w_ref[...])
        gh = jnp.where(j < q_tiles, gq_ref[...], gk_ref[...])
        for hh in range(tn // SB_HEAD_DIM):
            sl = slice(hh * SB_HEAD_DIM, (hh + 1) * SB_HEAD_DIM)
            o_ref[:, sl] = _rms(acc[:, sl], gh).astype(BF16)

    @pl.when(j >= k_tiles)
    def _():
        o_ref[...] = _dot(hn_ref[...], w_ref[...]).astype(BF16)


def _qkv_mm(h, g, w, layer, gq, gk, *, tm, tn):
    m, d = h.shape
    n = w.shape[2]
    width = n // 3
    tm = min(tm, m)
    assert width % tn == 0
    return pl.pallas_call(
        functools.partial(_qkv_kernel, tn=tn, q_tiles=width // tn, k_tiles=2 * width // tn),
        grid=(m // tm, n // tn),
        in_specs=[
            pl.BlockSpec((tm, d), lambda i, j: (i, 0)),
            pl.BlockSpec((1, d), lambda i, j: (0, 0)),
            pl.BlockSpec((None, d, tn), lambda i, j: (layer, 0, j)),
            pl.BlockSpec((1, SB_HEAD_DIM), lambda i, j: (0, 0)),
            pl.BlockSpec((1, SB_HEAD_DIM), lambda i, j: (0, 0)),
        ],
        out_specs=pl.BlockSpec((tm, tn), lambda i, j: (i, j)),
        out_shape=jax.ShapeDtypeStruct((m, n), BF16),
        scratch_shapes=[pltpu.VMEM((tm, d), BF16)],
        compiler_params=_params("parallel", "arbitrary"),
        name="qkv_mm",
    )(h, g.reshape(1, d), w, gq.reshape(1, -1), gk.reshape(1, -1))


def _res_mm_kernel(a_ref, w_ref, r_ref, o_ref, *, nk):
    if nk == 1:
        o_ref[...] = r_ref[...] + _dot(a_ref[...], w_ref[...])
        return
    k = pl.program_id(2)

    @pl.when(k == 0)
    def _():
        o_ref[...] = _dot(a_ref[...], w_ref[...])

    @pl.when((k > 0) & (k < nk - 1))
    def _():
        o_ref[...] += _dot(a_ref[...], w_ref[...])

    @pl.when(k == nk - 1)
    def _():
        o_ref[...] = r_ref[...] + (o_ref[...] + _dot(a_ref[...], w_ref[...]))


def _res_mm(a, w, layer, res, *, tm, tn, tk):
    m, kdim = a.shape
    n = w.shape[2]
    tm, tn, tk = min(tm, m), min(tn, n), min(tk, kdim)
    nk = kdim // tk
    return pl.pallas_call(
        functools.partial(_res_mm_kernel, nk=nk),
        grid=(m // tm, n // tn, nk),
        in_specs=[
            pl.BlockSpec((tm, tk), lambda i, j, k: (i, k)),
            pl.BlockSpec((None, tk, tn), lambda i, j, k: (layer, k, j)),
            pl.BlockSpec((tm, tn), lambda i, j, k: (i, j)),
        ],
        out_specs=pl.BlockSpec((tm, tn), lambda i, j, k: (i, j)),
        out_shape=jax.ShapeDtypeStruct((m, n), F32),
        compiler_params=_params("parallel", "parallel", "arbitrary"),
        name="res_mm",
    )(a, w, res)


def _ple_kernel(h_ref, p_ref, gg_ref, wg_ref, wp_ref, gp_ref, o_ref, hn_ref, e_ref, *, tn):
    j = pl.program_id(1)

    def gated(col):
        gate = jax.nn.sigmoid(_dot(hn_ref[...], wg_ref[...]))
        o_ref[...] = h_ref[:, pl.ds(col, tn)] + gate * e_ref[:, pl.ds(col, tn)]

    @pl.when(j == 0)
    def _():
        hn_ref[...] = _rms(h_ref[...], gg_ref[...]).astype(BF16)
        e_ref[...] = _rms(_dot(p_ref[...].astype(BF16), wp_ref[...]), gp_ref[...])
        gated(0)

    @pl.when(j > 0)
    def _():
        gated(pl.multiple_of(j * tn, tn))


def _ple(h, p, layer, gg, wg, wp, gp, *, tm, tn):
    m, d = h.shape
    pd = p.shape[2]
    tm, tn = min(tm, m), min(tn, d)
    return pl.pallas_call(
        functools.partial(_ple_kernel, tn=tn),
        grid=(m // tm, d // tn),
        in_specs=[
            pl.BlockSpec((tm, d), lambda i, j: (i, 0)),
            pl.BlockSpec((None, tm, pd), lambda i, j: (layer, i, 0)),
            pl.BlockSpec((1, d), lambda i, j: (0, 0)),
            pl.BlockSpec((None, d, tn), lambda i, j: (layer, 0, j)),
            pl.BlockSpec((None, pd, d), lambda i, j: (layer, 0, 0)),
            pl.BlockSpec((1, d), lambda i, j: (0, 0)),
        ],
        out_specs=pl.BlockSpec((tm, tn), lambda i, j: (i, j)),
        out_shape=jax.ShapeDtypeStruct((m, d), F32),
        scratch_shapes=[pltpu.VMEM((tm, d), BF16), pltpu.VMEM((tm, d), F32)],
        compiler_params=_params("parallel", "arbitrary"),
        name="ple",
    )(h, p, gg.reshape(1, d), wg, wp, gp.reshape(1, d))


def _attn_kernel(q_ref, k_ref, v_ref, t_ref, o_ref, *, blk, heads, scale):
    qi = pl.program_id(2)
    dh = SB_HEAD_DIM
    tri2 = t_ref[...]
    qs = [q_ref[:, hh * dh:(hh + 1) * dh] for hh in range(heads)]

    def tiles(kb, carries, masked):
        start = pl.multiple_of(kb * blk, blk)
        hs = range(heads)
        ks = [k_ref[pl.ds(start, blk), hh * dh:(hh + 1) * dh] for hh in hs]
        zs = [_dot_nt(qs[hh], ks[hh]) * (scale * LOG2E) for hh in hs]
        sps = [_softplus2(z) for z in zs]
        spms = sps
        if masked:
            t_idx = lax.broadcasted_iota(jnp.int32, (blk, blk), 0)
            s_idx = lax.broadcasted_iota(jnp.int32, (blk, blk), 1)
            mask = s_idx < t_idx
            spms = [jnp.where(mask, sp, 0.0) for sp in sps]
        splits = [_split2(spm) for spm in spms]
        sufs = [_dot(jnp.concatenate([hi, lo], axis=1), tri2) for hi, lo in splits]
        a_s = [jnp.exp2((zs[hh] - sps[hh]) - sufs[hh] + carries[hh]) for hh in hs]
        if masked:
            a_s = [jnp.where(mask, a, 0.0) for a in a_s]
        vs = [v_ref[pl.ds(start, blk), hh * dh:(hh + 1) * dh] for hh in hs]
        pvs = [_dot(a_s[hh].astype(BF16), vs[hh]) for hh in hs]
        new_carries = [carries[hh] - jnp.sum(spms[hh], axis=-1, keepdims=True) for hh in hs]
        return pvs, new_carries

    accs, carries = tiles(qi, [jnp.zeros((blk, 1), F32)] * heads, True)

    def body(it, state):
        accs, carries = state[:heads], state[heads:]
        pvs, carries = tiles(qi - 1 - it, carries, False)
        return tuple(acc + pv for acc, pv in zip(accs, pvs)) + tuple(carries)

    state = lax.fori_loop(0, qi, body, tuple(accs) + tuple(carries))
    for hh in range(heads):
        o_ref[:, hh * dh:(hh + 1) * dh] = state[hh].astype(o_ref.dtype)


def _attention(qkv, batch, seq):
    blk = min(ATTN_BLOCK, seq)
    nq = seq // blk
    hg = SB_HEADS // ATTN_HEADS_PER_STEP
    gw = ATTN_HEADS_PER_STEP * SB_HEAD_DIM
    tri = (lax.broadcasted_iota(jnp.int32, (blk, blk), 0)
           > lax.broadcasted_iota(jnp.int32, (blk, blk), 1)).astype(BF16)
    tri2 = jnp.concatenate([tri, tri], axis=0)
    return pl.pallas_call(
        functools.partial(_attn_kernel, blk=blk, heads=ATTN_HEADS_PER_STEP, scale=SB_HEAD_DIM ** -0.5),
        grid=(batch, hg, nq),
        in_specs=[
            pl.BlockSpec((blk, gw), lambda b, g, i: (b * nq + i, g)),
            pl.BlockSpec((seq, gw), lambda b, g, i: (b, hg + g)),
            pl.BlockSpec((seq, gw), lambda b, g, i: (b, 2 * hg + g)),
            pl.BlockSpec((2 * blk, blk), lambda b, g, i: (0, 0)),
        ],
        out_specs=pl.BlockSpec((blk, gw), lambda b, g, i: (b * nq + i, g)),
        out_shape=jax.ShapeDtypeStruct((batch * seq, SB_HEADS * SB_HEAD_DIM), BF16),
        compiler_params=_params("parallel", "parallel", "arbitrary"),
        name="stickbreak_attn",
    )(qkv, qkv, qkv, tri2)


def _inproj_kernel(h_ref, g_ref, w_ref, cw_ref, b_ref, o_ref, hn_ref, tail_ref, *,
                   tm, nj, z_tiles, tiles_per_seq):
    i, j = pl.program_id(0), pl.program_id(1)

    @pl.when(j == 0)
    def _():
        hn_ref[...] = _rms(h_ref[...], g_ref[...]).astype(BF16)

    @pl.when((j == 0) & (i % tiles_per_seq == 0))
    def _():
        tail_ref[...] = jnp.zeros_like(tail_ref)

    @pl.when(j < z_tiles)
    def _():
        o_ref[...] = _silu(_dot(hn_ref[...], w_ref[...]))

    @pl.when((j >= z_tiles) & (j < nj - 1))
    def _():
        acc = _dot(hn_ref[...], w_ref[...])
        ext = jnp.concatenate([tail_ref[j - z_tiles], acc], axis=0)
        out = acc * cw_ref[SSM_CONV - 1:SSM_CONV, :] + b_ref[...]
        for s in range(1, SSM_CONV):
            out = out + ext[8 - s:8 - s + tm, :] * cw_ref[SSM_CONV - 1 - s:SSM_CONV - s, :]
        o_ref[...] = _silu(out)
        tail_ref[j - z_tiles] = acc[tm - 8:tm, :]

    @pl.when(j == nj - 1)
    def _():
        o_ref[...] = _softplus(_dot(hn_ref[...], w_ref[...]) + b_ref[...])


def _inproj(h, g, w, layer, conv_w, bias, seq, *, tm, tn):
    m, d = h.shape
    n = w.shape[2]
    tm = min(tm, m, seq)
    inner = SSM_GROUPS * SSM_GROUP_WIDTH
    ni, nj = m // tm, n // tn
    assert seq % tm == 0 and inner % tn == 0 and n - tn == 2 * inner + 2 * SSM_GROUPS * SSM_STATE
    z_tiles = inner // tn
    kern = functools.partial(_inproj_kernel, tm=tm, nj=nj, z_tiles=z_tiles, tiles_per_seq=seq // tm)
    return pl.pallas_call(
        kern,
        grid=(ni, nj),
        in_specs=[
            pl.BlockSpec((tm, d), lambda i, j: (i, 0)),
            pl.BlockSpec((1, d), lambda i, j: (0, 0)),
            pl.BlockSpec((None, d, tn), lambda i, j: (layer, 0, j)),
            pl.BlockSpec((SSM_CONV, tn), lambda i, j: (0, j)),
            pl.BlockSpec((1, tn), lambda i, j: (0, j)),
        ],
        out_specs=pl.BlockSpec((tm, tn), lambda i, j: (i, j)),
        out_shape=jax.ShapeDtypeStruct((m, n), F32),
        scratch_shapes=[pltpu.VMEM((tm, d), BF16),
                        pltpu.VMEM((nj - 1 - z_tiles, 8, tn), F32)],
        compiler_params=_params("arbitrary", "arbitrary"),
        name="ssm_inproj",
    )(h, g.reshape(1, d), w, conv_w, bias)


def _ssd_kernel(z_ref, x_ref, b_ref, c_ref, dt_ref, alog_ref, d_ref, g_ref, tri_ref, exp_ref,
                o_ref, state_ref, *, q, groups):
    @pl.when(pl.program_id(2) == 0)
    def _():
        state_ref[...] = jnp.zeros_like(state_ref)

    gw, ns, gs = SSM_GROUP_WIDTH, SSM_STATE, range(groups)
    pairs = range(SSM_HEADS_PER_GROUP // 2)
    tri = tri_ref[...]
    expand2 = exp_ref[...]
    i_idx = lax.broadcasted_iota(jnp.int32, (q, q), 0)
    j_idx = lax.broadcasted_iota(jnp.int32, (q, q), 1)
    causal = j_idx <= i_idx
    left2 = lax.broadcasted_iota(jnp.int32, (2 * q, LANES), 1) < SSM_HEAD_DIM
    top2 = lax.broadcasted_iota(jnp.int32, (2 * q, LANES), 0) < q
    block_diag = left2 == top2

    def expand(v):
        hi, lo = _split2(v)
        return _dot(jnp.concatenate([hi, lo], axis=1), expand2)

    xs = [x_ref[:, g * gw:(g + 1) * gw] for g in gs]
    bm16 = [b_ref[:, g * ns:(g + 1) * ns].astype(BF16) for g in gs]
    cm16 = [c_ref[:, g * ns:(g + 1) * ns].astype(BF16) for g in gs]
    dts = [dt_ref[:, g * LANES:(g + 1) * LANES] for g in gs]
    das = [dts[g] * (-jnp.exp(alog_ref[:, g * LANES:(g + 1) * LANES])) for g in gs]
    splits = [_split3(da) for da in das]
    acums = [_dot(tri, hi) + _dot(tri, mid) + _dot(tri, lo) for hi, mid, lo in splits]
    states = [state_ref[g] for g in gs]
    cbs = [_dot_nt(cm16[g], bm16[g]) for g in gs]
    y_offs = [_dot(cm16[g], states[g].astype(BF16)) for g in gs]
    acum_ts = [acum.T for acum in acums]
    dt_ts = [dt.T for dt in dts]
    grows = [expand(jnp.exp(acum)) for acum in acums]
    tails = [expand(jnp.exp(acums[g][q - 1:q, :] - acums[g]) * dts[g]) for g in gs]

    y_diags = [[None] * len(pairs) for g in gs]
    for pair in pairs:
        sl = slice(pair * LANES, (pair + 1) * LANES)
        m_cats, x_bds = [], []
        for g in gs:
            mats = []
            for hh in (2 * pair, 2 * pair + 1):
                diff = acums[g][:, hh:hh + 1] - acum_ts[g][hh:hh + 1, :]
                decay = jnp.exp(jnp.where(causal, diff, NEG_BIG))
                mats.append((cbs[g] * decay * dt_ts[g][hh:hh + 1, :]).astype(BF16))
            m_cats.append(jnp.concatenate(mats, axis=1))
            xp = xs[g][:, sl]
            x_bds.append(jnp.where(block_diag, jnp.concatenate([xp, xp], axis=0), 0.0).astype(BF16))
        for g in gs:
            y_diags[g][pair] = _dot(m_cats[g], x_bds[g])

    xws = [(xs[g] * tails[g]).astype(BF16) for g in gs]
    updates = [_dot_tn(bm16[g], xws[g]) for g in gs]
    for g in gs:
        wide = slice(g * gw, (g + 1) * gw)
        state_ref[g] = states[g] * grows[g][q - 1:q, :] + updates[g]
        y = jnp.concatenate(y_diags[g], axis=1) + y_offs[g] * grows[g] + d_ref[:, wide] * xs[g]
        o_ref[:, wide] = _rms(y * z_ref[:, wide], g_ref[:, wide]).astype(o_ref.dtype)


def _ssd(zx, batch, seq, a_log, d_skip, g_out):
    q = min(SSD_CHUNK, seq)
    nc = seq // q
    gs = SSD_GROUPS_PER_STEP
    gw, ns, g = gs * SSM_GROUP_WIDTH, gs * SSM_STATE, SSM_GROUPS
    inner = SSM_GROUPS * SSM_GROUP_WIDTH
    x0, b0 = inner // gw, 2 * inner // ns
    c0, dt0 = b0 + g // gs, b0 + 2 * (g // gs)

    def row(b, gg, c):
        return b * nc + c

    tri = (lax.broadcasted_iota(jnp.int32, (q, q), 1)
           <= lax.broadcasted_iota(jnp.int32, (q, q), 0)).astype(BF16)
    expand = (lax.broadcasted_iota(jnp.int32, (LANES, SSM_GROUP_WIDTH), 0)
              == lax.broadcasted_iota(jnp.int32, (LANES, SSM_GROUP_WIDTH), 1) // SSM_HEAD_DIM).astype(BF16)
    expand2 = jnp.concatenate([expand, expand], axis=0)
    in_specs = [
        pl.BlockSpec((q, gw), lambda b, gg, c: (row(b, gg, c), gg)),
        pl.BlockSpec((q, gw), lambda b, gg, c: (row(b, gg, c), x0 + gg)),
        pl.BlockSpec((q, ns), lambda b, gg, c: (row(b, gg, c), b0 + gg)),
        pl.BlockSpec((q, ns), lambda b, gg, c: (row(b, gg, c), c0 + gg)),
        pl.BlockSpec((q, gs * LANES), lambda b, gg, c: (row(b, gg, c), dt0 + gg)),
        pl.BlockSpec((1, gs * LANES), lambda b, gg, c: (0, gg)),
        pl.BlockSpec((1, gw), lambda b, gg, c: (0, gg)),
        pl.BlockSpec((1, gw), lambda b, gg, c: (0, gg)),
        pl.BlockSpec((q, q), lambda b, gg, c: (0, 0)),
        pl.BlockSpec((2 * LANES, SSM_GROUP_WIDTH), lambda b, gg, c: (0, 0)),
    ]
    return pl.pallas_call(
        functools.partial(_ssd_kernel, q=q, groups=gs),
        grid=(batch, g // gs, nc),
        in_specs=in_specs,
        out_specs=pl.BlockSpec((q, gw), lambda b, gg, c: (row(b, gg, c), gg)),
        out_shape=jax.ShapeDtypeStruct((batch * seq, inner), BF16),
        scratch_shapes=[pltpu.VMEM((gs, SSM_STATE, SSM_GROUP_WIDTH), F32)],
        compiler_params=_params("parallel", "parallel", "arbitrary"),
        name="ssd",
    )(zx, zx, zx, zx, zx, a_log, d_skip, g_out.reshape(1, -1), tri, expand2)


def _pad_heads(v):
    v = v.reshape(SSM_GROUPS, SSM_HEADS_PER_GROUP)
    v = jnp.pad(v, ((0, 0), (0, LANES - SSM_HEADS_PER_GROUP)))
    return v.reshape(1, SSM_GROUPS * LANES)


def _pad_dt_columns(w_in):
    lead = w_in.shape[:-1]
    heads = SSM_GROUPS * SSM_HEADS_PER_GROUP
    main, w_dt = w_in[..., :-heads], w_in[..., -heads:]
    w_dt = w_dt.reshape(*lead, SSM_GROUPS, SSM_HEADS_PER_GROUP)
    w_dt = jnp.pad(w_dt, [(0, 0)] * (len(lead) + 1) + [(0, LANES - SSM_HEADS_PER_GROUP)])
    return jnp.concatenate([main, w_dt.reshape(*lead, SSM_GROUPS * LANES)], axis=-1)


TILES_INPROJ = dict(tm=1024, tn=1024)
TILES_QKV = dict(tm=1024, tn=2048)
TILES_MLP_UP = dict(tm=1024, tn=2048)
TILES_W_O = dict(tm=512, tn=2048, tk=2048)
TILES_SSM_OUT = dict(tm=1024, tn=2048, tk=1024)
TILES_MLP_DOWN = dict(tm=1024, tn=2048, tk=1024)
TILES_PLE = dict(tm=512, tn=1024)


def kernel(x, p, attn_norm, attn_w_qkv, attn_q_norm, attn_k_norm, attn_w_o, ssm_norm, ssm_w_in, ssm_conv_w, ssm_conv_b, ssm_dt_bias, ssm_a_log, ssm_d, ssm_out_norm, ssm_w_out, mlp_norm, mlp_w_up, mlp_w_down, ple_w_proj, ple_norm, ple_gate_norm, ple_w_gate):
    batch, seq, d = x.shape
    m = batch * seq
    h = x.reshape(m, d)
    p = p.reshape(p.shape[0], m, PLE_DIM)
    w_qkv, w_o = attn_w_qkv.astype(BF16), attn_w_o.astype(BF16)
    w_in, w_out = _pad_dt_columns(ssm_w_in).astype(BF16), ssm_w_out.astype(BF16)
    w_up, w_down = mlp_w_up.astype(BF16), mlp_w_down.astype(BF16)
    w_gate, w_proj = ple_w_gate.astype(BF16), ple_w_proj.astype(BF16)
    inner = SSM_GROUPS * SSM_GROUP_WIDTH
    for i in range(DEPTH):
        j = i // 2
        if i % 2 == 0:
            qkv = _qkv_mm(h, attn_norm[j], w_qkv, j, attn_q_norm[j], attn_k_norm[j], **TILES_QKV)
            o = _attention(qkv, batch, seq)
            h = _res_mm(o, w_o, j, h, **TILES_W_O)
        else:
            conv_w = jnp.pad(ssm_conv_w[j], ((0, 0), (inner, SSM_GROUPS * LANES)))
            bias = jnp.concatenate([jnp.zeros((1, inner), F32), ssm_conv_b[j].reshape(1, -1),
                                    _pad_heads(ssm_dt_bias[j])], axis=1)
            zx = _inproj(h, ssm_norm[j], w_in, j, conv_w, bias, seq, **TILES_INPROJ)
            y = _ssd(zx, batch, seq, _pad_heads(ssm_a_log[j]),
                     jnp.repeat(ssm_d[j], SSM_HEAD_DIM).reshape(1, -1), ssm_out_norm[j])
            h = _res_mm(y, w_out, j, h, **TILES_SSM_OUT)
        u = _norm_mm(h, mlp_norm[i], w_up, i, out_dtype=BF16, **TILES_MLP_UP)
        h = _res_mm(u, w_down, i, h, **TILES_MLP_DOWN)
        h = _ple(h, p, i, ple_gate_norm[i], w_gate, w_proj, ple_norm[i], **TILES_PLE)
    return h.reshape(batch, seq, d)
```

```python
import functools

import jax
import jax.numpy as jnp
from jax import lax
from jax.experimental import pallas as pl
from jax.experimental.pallas import tpu as pltpu

F32 = jnp.float32
BF16 = jnp.bfloat16

NORM_EPS = 1e-6
LANES = 128
VMEM_LIMIT = 56 * 1024 * 1024

DEPTH = 4
PLE_DIM = 256
SB_HEADS = 16
SB_HEAD_DIM = 128
SSM_HEAD_DIM = 64
SSM_GROUPS = 8
SSM_HEADS_PER_GROUP = 8
SSM_STATE = 128
SSM_CONV = 4
SSM_GROUP_WIDTH = SSM_HEADS_PER_GROUP * SSM_HEAD_DIM
SSD_CHUNK = 128
SSD_GROUPS_PER_STEP = 8
ATTN_BLOCK = 256
ATTN_HEADS_PER_STEP = 8
LOG2E = 1.4426950408889634
NEG_BIG = -1e30


def _params(*sem):
    return pltpu.CompilerParams(dimension_semantics=sem, vmem_limit_bytes=VMEM_LIMIT)


def _rms(x, g):
    ms = jnp.mean(x * x, axis=-1, keepdims=True)
    return x * lax.rsqrt(ms + NORM_EPS) * g


def _softplus(x):
    return jnp.maximum(x, 0.0) + jnp.log(1.0 + jnp.exp(-jnp.abs(x)))


def _softplus2(x):
    neg_abs = lax.bitcast_convert_type(
        lax.bitcast_convert_type(x, jnp.uint32) | jnp.uint32(0x80000000), F32)
    return jnp.maximum(x, 0.0) + jnp.log2(1.0 + jnp.exp2(neg_abs))


def _silu(x):
    return x * jax.nn.sigmoid(x)


def _split2(x):
    hi = x.astype(BF16)
    lo = (x - hi.astype(F32)).astype(BF16)
    return hi, lo


def _split3(x):
    hi = x.astype(BF16)
    r1 = x - hi.astype(F32)
    mid = r1.astype(BF16)
    lo = (r1 - mid.astype(F32)).astype(BF16)
    return hi, mid, lo


def _dot(a, b):
    return jnp.dot(a, b, preferred_element_type=F32)


def _dot_nt(a, b):
    return lax.dot_general(a, b, (((1,), (1,)), ((), ())), preferred_element_type=F32)


def _dot_tn(a, b):
    return lax.dot_general(a, b, (((0,), (0,)), ((), ())), preferred_element_type=F32)


def _norm_mm_kernel(h_ref, g_ref, w_ref, o_ref, hn_ref):
    @pl.when(pl.program_id(1) == 0)
    def _():
        hn_ref[...] = _rms(h_ref[...], g_ref[...]).astype(BF16)

    r = jnp.maximum(_dot(hn_ref[...], w_ref[...]), 0.0)
    o_ref[...] = (r * r).astype(o_ref.dtype)


def _norm_mm(h, g, w, layer, *, out_dtype, tm, tn):
    m, d = h.shape
    n = w.shape[2]
    tm, tn = min(tm, m), min(tn, n)
    return pl.pallas_call(
        _norm_mm_kernel,
        grid=(m // tm, n // tn),
        in_specs=[
            pl.BlockSpec((tm, d), lambda i, j: (i, 0)),
            pl.BlockSpec((1, d), lambda i, j: (0, 0)),
            pl.BlockSpec((None, d, tn), lambda i, j: (layer, 0, j)),
        ],
        out_specs=pl.BlockSpec((tm, tn), lambda i, j: (i, j)),
        out_shape=jax.ShapeDtypeStruct((m, n), out_dtype),
        scratch_shapes=[pltpu.VMEM((tm, d), BF16)],
        compiler_params=_params("parallel", "arbitrary"),
        name="norm_mm_relu2",
    )(h, g.reshape(1, d), w)


def _qkv_kernel(h_ref, g_ref, w_ref, gq_ref, gk_ref, o_ref, hn_ref, *, tn, q_tiles, k_tiles):
    j = pl.program_id(1)

    @pl.when(j == 0)
    def _():
        hn_ref[...] = _rms(h_ref[...], g_ref[...]).astype(BF16)

    @pl.when(j < k_tiles)
    def _():
        acc = _dot(hn_ref[...], w_ref[...])
        gh = jnp.where(j < q_tiles, gq_ref[...], gk_ref[...])
        for hh in range(tn // SB_HEAD_DIM):
            sl = slice(hh * SB_HEAD_DIM, (hh + 1) * SB_HEAD_DIM)
            o_ref[:, sl] = _rms(acc[:, sl], gh).astype(BF16)

    @pl.when(j >= k_tiles)
    def _():
        o_ref[...] = _dot(hn_ref[...], w_ref[...]).astype(BF16)


def _qkv_mm(h, g, w, layer, gq, gk, *, tm, tn):
    m, d = h.shape
    n = w.shape[2]
    width = n // 3
    tm = min(tm, m)
    assert width % tn == 0
    return pl.pallas_call(
        functools.partial(_qkv_kernel, tn=tn, q_tiles=width // tn, k_tiles=2 * width // tn),
        grid=(m // tm, n // tn),
        in_specs=[
            pl.BlockSpec((tm, d), lambda i, j: (i, 0)),
            pl.BlockSpec((1, d), lambda i, j: (0, 0)),
            pl.BlockSpec((None, d, tn), lambda i, j: (layer, 0, j)),
            pl.BlockSpec((1, SB_HEAD_DIM), lambda i, j: (0, 0)),
            pl.BlockSpec((1, SB_HEAD_DIM), lambda i, j: (0, 0)),
        ],
        out_specs=pl.BlockSpec((tm, tn), lambda i, j: (i, j)),
        out_shape=jax.ShapeDtypeStruct((m, n), BF16),
        scratch_shapes=[pltpu.VMEM((tm, d), BF16)],
        compiler_params=_params("parallel", "arbitrary"),
        name="qkv_mm",
    )(h, g.reshape(1, d), w, gq.reshape(1, -1), gk.reshape(1, -1))


def _res_mm_kernel(a_ref, w_ref, r_ref, o_ref, *, nk):
    if nk == 1:
        o_ref[...] = r_ref[...] + _dot(a_ref[...], w_ref[...])
        return
    k = pl.program_id(2)

    @pl.when(k == 0)
    def _():
        o_ref[...] = _dot(a_ref[...], w_ref[...])

    @pl.when((k > 0) & (k < nk - 1))
    def _():
        o_ref[...] += _dot(a_ref[...], w_ref[...])

    @pl.when(k == nk - 1)
    def _():
        o_ref[...] = r_ref[...] + (o_ref[...] + _dot(a_ref[...], w_ref[...]))


def _res_mm(a, w, layer, res, *, tm, tn, tk):
    m, kdim = a.shape
    n = w.shape[2]
    tm, tn, tk = min(tm, m), min(tn, n), min(tk, kdim)
    nk = kdim // tk
    return pl.pallas_call(
        functools.partial(_res_mm_kernel, nk=nk),
        grid=(m // tm, n // tn, nk),
        in_specs=[
            pl.BlockSpec((tm, tk), lambda i, j, k: (i, k)),
            pl.BlockSpec((None, tk, tn), lambda i, j, k: (layer, k, j)),
            pl.BlockSpec((tm, tn), lambda i, j, k: (i, j)),
        ],
        out_specs=pl.BlockSpec((tm, tn), lambda i, j, k: (i, j)),
        out_shape=jax.ShapeDtypeStruct((m, n), F32),
        compiler_params=_params("parallel", "parallel", "arbitrary"),
        name="res_mm",
    )(a, w, res)


def _ple_kernel(h_ref, p_ref, gg_ref, wg_ref, wp_ref, gp_ref, o_ref, hn_ref, e_ref, *, tn):
    j = pl.program_id(1)

    def gated(col):
        gate = jax.nn.sigmoid(_dot(hn_ref[...], wg_ref[...]))
        o_ref[...] = h_ref[:, pl.ds(col, tn)] + gate * e_ref[:, pl.ds(col, tn)]

    @pl.when(j == 0)
    def _():
        hn_ref[...] = _rms(h_ref[...], gg_ref[...]).astype(BF16)
        e_ref[...] = _rms(_dot(p_ref[...].astype(BF16), wp_ref[...]), gp_ref[...])
        gated(0)

    @pl.when(j > 0)
    def _():
        gated(pl.multiple_of(j * tn, tn))


def _ple(h, p, layer, gg, wg, wp, gp, *, tm, tn):
    m, d = h.shape
    pd = p.shape[2]
    tm, tn = min(tm, m), min(tn, d)
    return pl.pallas_call(
        functools.partial(_ple_kernel, tn=tn),
        grid=(m // tm, d // tn),
        in_specs=[
            pl.BlockSpec((tm, d), lambda i, j: (i, 0)),
            pl.BlockSpec((None, tm, pd), lambda i, j: (layer, i, 0)),
            pl.BlockSpec((1, d), lambda i, j: (0, 0)),
            pl.BlockSpec((None, d, tn), lambda i, j: (layer, 0, j)),
            pl.BlockSpec((None, pd, d), lambda i, j: (layer, 0, 0)),
            pl.BlockSpec((1, d), lambda i, j: (0, 0)),
        ],
        out_specs=pl.BlockSpec((tm, tn), lambda i, j: (i, j)),
        out_shape=jax.ShapeDtypeStruct((m, d), F32),
        scratch_shapes=[pltpu.VMEM((tm, d), BF16), pltpu.VMEM((tm, d), F32)],
        compiler_params=_params("parallel", "arbitrary"),
        name="ple",
    )(h, p, gg.reshape(1, d), wg, wp, gp.reshape(1, d))


def _attn_kernel(q_ref, k_ref, v_ref, t_ref, o_ref, *, blk, heads, scale):
    qi = pl.program_id(2)
    dh = SB_HEAD_DIM
    tri2 = t_ref[...]
    qs = [q_ref[:, hh * dh:(hh + 1) * dh] for hh in range(heads)]

    def tiles(kb, carries, masked):
        start = pl.multiple_of(kb * blk, blk)
        hs = range(heads)
        ks = [k_ref[pl.ds(start, blk), hh * dh:(hh + 1) * dh] for hh in hs]
        zs = [_dot_nt(qs[hh], ks[hh]) * (scale * LOG2E) for hh in hs]
        sps = [_softplus2(z) for z in zs]
        spms = sps
        if masked:
            t_idx = lax.broadcasted_iota(jnp.int32, (blk, blk), 0)
            s_idx = lax.broadcasted_iota(jnp.int32, (blk, blk), 1)
            mask = s_idx < t_idx
            spms = [jnp.where(mask, sp, 0.0) for sp in sps]
        splits = [_split2(spm) for spm in spms]
        sufs = [_dot(jnp.concatenate([hi, lo], axis=1), tri2) for hi, lo in splits]
        a_s = [jnp.exp2((zs[hh] - sps[hh]) - sufs[hh] + carries[hh]) for hh in hs]
        if masked:
            a_s = [jnp.where(mask, a, 0.0) for a in a_s]
        vs = [v_ref[pl.ds(start, blk), hh * dh:(hh + 1) * dh] for hh in hs]
        pvs = [_dot(a_s[hh].astype(BF16), vs[hh]) for hh in hs]
        new_carries = [carries[hh] - jnp.sum(spms[hh], axis=-1, keepdims=True) for hh in hs]
        return pvs, new_carries

    accs, carries = tiles(qi, [jnp.zeros((blk, 1), F32)] * heads, True)

    def body(it, state):
        accs, carries = state[:heads], state[heads:]
        pvs, carries = tiles(qi - 1 - it, carries, False)
        return tuple(acc + pv for acc, pv in zip(accs, pvs)) + tuple(carries)

    state = lax.fori_loop(0, qi, body, tuple(accs) + tuple(carries))
    for hh in range(heads):
        o_ref[:, hh * dh:(hh + 1) * dh] = state[hh].astype(o_ref.dtype)


def _attention(qkv, batch, seq):
    blk = min(ATTN_BLOCK, seq)
    nq = seq // blk
    hg = SB_HEADS // ATTN_HEADS_PER_STEP
    gw = ATTN_HEADS_PER_STEP * SB_HEAD_DIM
    tri = (lax.broadcasted_iota(jnp.int32, (blk, blk), 0)
           > lax.broadcasted_iota(jnp.int32, (blk, blk), 1)).astype(BF16)
    tri2 = jnp.concatenate([tri, tri], axis=0)
    return pl.pallas_call(
        functools.partial(_attn_kernel, blk=blk, heads=ATTN_HEADS_PER_STEP, scale=SB_HEAD_DIM ** -0.5),
        grid=(batch, hg, nq),
        in_specs=[
            pl.BlockSpec((blk, gw), lambda b, g, i: (b * nq + i, g)),
            pl.BlockSpec((seq, gw), lambda b, g, i: (b, hg + g)),
            pl.BlockSpec((seq, gw), lambda b, g, i: (b, 2 * hg + g)),
            pl.BlockSpec((2 * blk, blk), lambda b, g, i: (0, 0)),
        ],
        out_specs=pl.BlockSpec((blk, gw), lambda b, g, i: (b * nq + i, g)),
        out_shape=jax.ShapeDtypeStruct((batch * seq, SB_HEADS * SB_HEAD_DIM), BF16),
        compiler_params=_params("parallel", "parallel", "arbitrary"),
        name="stickbreak_attn",
    )(qkv, qkv, qkv, tri2)


def _inproj_kernel(h_ref, g_ref, w_ref, wdt_ref, cw_ref, b_ref, o_ref, hn_ref, tail_ref, *,
                   tm, nj, z_tiles, tiles_per_seq):
    i, j = pl.program_id(0), pl.program_id(1)

    @pl.when(j == 0)
    def _():
        hn_ref[...] = _rms(h_ref[...], g_ref[...]).astype(BF16)

    @pl.when((j == 0) & (i % tiles_per_seq == 0))
    def _():
        tail_ref[...] = jnp.zeros_like(tail_ref)

    @pl.when(j < z_tiles)
    def _():
        o_ref[...] = _silu(_dot(hn_ref[...], w_ref[...]))

    @pl.when((j >= z_tiles) & (j < nj - 1))
    def _():
        acc = _dot(hn_ref[...], w_ref[...])
        ext = jnp.concatenate([tail_ref[j - z_tiles], acc], axis=0)
        out = acc * cw_ref[SSM_CONV - 1:SSM_CONV, :] + b_ref[...]
        for s in range(1, SSM_CONV):
            out = out + ext[8 - s:8 - s + tm, :] * cw_ref[SSM_CONV - 1 - s:SSM_CONV - s, :]
        o_ref[...] = _silu(out)
        tail_ref[j - z_tiles] = acc[tm - 8:tm, :]

    @pl.when(j == nj - 1)
    def _():
        o_ref[...] = _softplus(_dot(hn_ref[...], wdt_ref[...]) + b_ref[...])


def _inproj(h, g, w, w_dt, layer, conv_w, bias, seq, *, tm, tn):
    m, d = h.shape
    tm = min(tm, m, seq)
    inner = SSM_GROUPS * SSM_GROUP_WIDTH
    n = 2 * inner + 2 * SSM_GROUPS * SSM_STATE + tn
    ni, nj = m // tm, n // tn
    assert seq % tm == 0 and inner % tn == 0 and n % tn == 0 and w_dt.shape[2] == tn
    z_tiles = inner // tn
    kern = functools.partial(_inproj_kernel, tm=tm, nj=nj, z_tiles=z_tiles, tiles_per_seq=seq // tm)
    return pl.pallas_call(
        kern,
        grid=(ni, nj),
        in_specs=[
            pl.BlockSpec((tm, d), lambda i, j: (i, 0)),
            pl.BlockSpec((1, d), lambda i, j: (0, 0)),
            pl.BlockSpec((None, d, tn), lambda i, j: (layer, 0, jnp.minimum(j, nj - 2))),
            pl.BlockSpec((None, d, tn), lambda i, j: (layer, 0, 0)),
            pl.BlockSpec((SSM_CONV, tn), lambda i, j: (0, j)),
            pl.BlockSpec((1, tn), lambda i, j: (0, j)),
        ],
        out_specs=pl.BlockSpec((tm, tn), lambda i, j: (i, j)),
        out_shape=jax.ShapeDtypeStruct((m, n), F32),
        scratch_shapes=[pltpu.VMEM((tm, d), BF16),
                        pltpu.VMEM((nj - 1 - z_tiles, 8, tn), F32)],
        compiler_params=_params("arbitrary", "arbitrary"),
        name="ssm_inproj",
    )(h, g.reshape(1, d), w, w_dt, conv_w, bias)


def _ssd_kernel(z_ref, x_ref, b_ref, c_ref, dt_ref, alog_ref, d_ref, g_ref, tri_ref, exp_ref,
                o_ref, state_ref, *, q, groups):
    @pl.when(pl.program_id(2) == 0)
    def _():
        state_ref[...] = jnp.zeros_like(state_ref)

    gw, ns, gs = SSM_GROUP_WIDTH, SSM_STATE, range(groups)
    pairs = range(SSM_HEADS_PER_GROUP // 2)
    tri = tri_ref[...]
    expand2 = exp_ref[...]
    i_idx = lax.broadcasted_iota(jnp.int32, (q, q), 0)
    j_idx = lax.broadcasted_iota(jnp.int32, (q, q), 1)
    causal = j_idx <= i_idx
    left2 = lax.broadcasted_iota(jnp.int32, (2 * q, LANES), 1) < SSM_HEAD_DIM
    top2 = lax.broadcasted_iota(jnp.int32, (2 * q, LANES), 0) < q
    block_diag = left2 == top2

    def expand(v):
        hi, lo = _split2(v)
        return _dot(jnp.concatenate([hi, lo], axis=1), expand2)

    xs = [x_ref[:, g * gw:(g + 1) * gw] for g in gs]
    bm16 = [b_ref[:, g * ns:(g + 1) * ns].astype(BF16) for g in gs]
    cm16 = [c_ref[:, g * ns:(g + 1) * ns].astype(BF16) for g in gs]
    dts = [dt_ref[:, g * LANES:(g + 1) * LANES] for g in gs]
    das = [dts[g] * (-jnp.exp(alog_ref[:, g * LANES:(g + 1) * LANES])) for g in gs]
    splits = [_split3(da) for da in das]
    acums = [_dot(tri, hi) + _dot(tri, mid) + _dot(tri, lo) for hi, mid, lo in splits]
    states = [state_ref[g] for g in gs]
    cbs = [_dot_nt(cm16[g], bm16[g]) for g in gs]
    y_offs = [_dot(cm16[g], states[g].astype(BF16)) for g in gs]
    acum_ts = [acum.T for acum in acums]
    dt_ts = [dt.T for dt in dts]
    grows = [expand(jnp.exp(acum)) for acum in acums]
    tails = [expand(jnp.exp(acums[g][q - 1:q, :] - acums[g]) * dts[g]) for g in gs]

    y_diags = [[None] * len(pairs) for g in gs]
    for pair in pairs:
        sl = slice(pair * LANES, (pair + 1) * LANES)
        m_cats, x_bds = [], []
        for g in gs:
            mats = []
            for hh in (2 * pair, 2 * pair + 1):
                diff = acums[g][:, hh:hh + 1] - acum_ts[g][hh:hh + 1, :]
                decay = jnp.exp(jnp.where(causal, diff, NEG_BIG))
                mats.append((cbs[g] * decay * dt_ts[g][hh:hh + 1, :]).astype(BF16))
            m_cats.append(jnp.concatenate(mats, axis=1))
            xp = xs[g][:, sl]
            x_bds.append(jnp.where(block_diag, jnp.concatenate([xp, xp], axis=0), 0.0).astype(BF16))
        for g in gs:
            y_diags[g][pair] = _dot(m_cats[g], x_bds[g])

    xws = [(xs[g] * tails[g]).astype(BF16) for g in gs]
    updates = [_dot_tn(bm16[g], xws[g]) for g in gs]
    for g in gs:
        wide = slice(g * gw, (g + 1) * gw)
        state_ref[g] = states[g] * grows[g][q - 1:q, :] + updates[g]
        y = jnp.concatenate(y_diags[g], axis=1) + y_offs[g] * grows[g] + d_ref[:, wide] * xs[g]
        o_ref[:, wide] = _rms(y * z_ref[:, wide], g_ref[:, wide]).astype(o_ref.dtype)


def _ssd(zx, batch, seq, a_log, d_skip, g_out):
    q = min(SSD_CHUNK, seq)
    nc = seq // q
    gs = SSD_GROUPS_PER_STEP
    gw, ns, g = gs * SSM_GROUP_WIDTH, gs * SSM_STATE, SSM_GROUPS
    inner = SSM_GROUPS * SSM_GROUP_WIDTH
    x0, b0 = inner // gw, 2 * inner // ns
    c0, dt0 = b0 + g // gs, b0 + 2 * (g // gs)

    def row(b, gg, c):
        return b * nc + c

    tri = (lax.broadcasted_iota(jnp.int32, (q, q), 1)
           <= lax.broadcasted_iota(jnp.int32, (q, q), 0)).astype(BF16)
    expand = (lax.broadcasted_iota(jnp.int32, (LANES, SSM_GROUP_WIDTH), 0)
              == lax.broadcasted_iota(jnp.int32, (LANES, SSM_GROUP_WIDTH), 1) // SSM_HEAD_DIM).astype(BF16)
    expand2 = jnp.concatenate([expand, expand], axis=0)
    in_specs = [
        pl.BlockSpec((q, gw), lambda b, gg, c: (row(b, gg, c), gg)),
        pl.BlockSpec((q, gw), lambda b, gg, c: (row(b, gg, c), x0 + gg)),
        pl.BlockSpec((q, ns), lambda b, gg, c: (row(b, gg, c), b0 + gg)),
        pl.BlockSpec((q, ns), lambda b, gg, c: (row(b, gg, c), c0 + gg)),
        pl.BlockSpec((q, gs * LANES), lambda b, gg, c: (row(b, gg, c), dt0 + gg)),
        pl.BlockSpec((1, gs * LANES), lambda b, gg, c: (0, gg)),
        pl.BlockSpec((1, gw), lambda b, gg, c: (0, gg)),
        pl.BlockSpec((1, gw), lambda b, gg, c: (0, gg)),
        pl.BlockSpec((q, q), lambda b, gg, c: (0, 0)),
        pl.BlockSpec((2 * LANES, SSM_GROUP_WIDTH), lambda b, gg, c: (0, 0)),
    ]
    return pl.pallas_call(
        functools.partial(_ssd_kernel, q=q, groups=gs),
        grid=(batch, g // gs, nc),
        in_specs=in_specs,
        out_specs=pl.BlockSpec((q, gw), lambda b, gg, c: (row(b, gg, c), gg)),
        out_shape=jax.ShapeDtypeStruct((batch * seq, inner), BF16),
        scratch_shapes=[pltpu.VMEM((gs, SSM_STATE, SSM_GROUP_WIDTH), F32)],
        compiler_params=_params("parallel", "parallel", "arbitrary"),
        name="ssd",
    )(zx, zx, zx, zx, zx, a_log, d_skip, g_out.reshape(1, -1), tri, expand2)


def _pad_heads(v):
    v = v.reshape(SSM_GROUPS, SSM_HEADS_PER_GROUP)
    v = jnp.pad(v, ((0, 0), (0, LANES - SSM_HEADS_PER_GROUP)))
    return v.reshape(1, SSM_GROUPS * LANES)


def _dt_columns(w_in):
    lead = w_in.shape[:-1]
    heads = SSM_GROUPS * SSM_HEADS_PER_GROUP
    w_dt = w_in[..., -heads:].reshape(*lead, SSM_GROUPS, SSM_HEADS_PER_GROUP)
    w_dt = jnp.pad(w_dt, [(0, 0)] * (len(lead) + 1) + [(0, LANES - SSM_HEADS_PER_GROUP)])
    return w_dt.reshape(*lead, SSM_GROUPS * LANES)


TILES_INPROJ = dict(tm=1024, tn=1024)
TILES_QKV = dict(tm=1024, tn=2048)
TILES_MLP_UP = dict(tm=1024, tn=2048)
TILES_W_O = dict(tm=512, tn=2048, tk=2048)
TILES_SSM_OUT = dict(tm=1024, tn=1024, tk=2048)
TILES_MLP_DOWN = dict(tm=1024, tn=1024, tk=2048)
TILES_PLE = dict(tm=1024, tn=512)


def kernel(x, p, attn_norm, attn_w_qkv, attn_q_norm, attn_k_norm, attn_w_o, ssm_norm, ssm_w_in, ssm_conv_w, ssm_conv_b, ssm_dt_bias, ssm_a_log, ssm_d, ssm_out_norm, ssm_w_out, mlp_norm, mlp_w_up, mlp_w_down, ple_w_proj, ple_norm, ple_gate_norm, ple_w_gate):
    batch, seq, d = x.shape
    m = batch * seq
    h = x.reshape(m, d)
    p = p.reshape(p.shape[0], m, PLE_DIM)
    w_qkv, w_o = attn_w_qkv.astype(BF16), attn_w_o.astype(BF16)
    w_in, w_out = ssm_w_in.astype(BF16), ssm_w_out.astype(BF16)
    w_dt = _dt_columns(ssm_w_in).astype(BF16)
    w_up, w_down = mlp_w_up.astype(BF16), mlp_w_down.astype(BF16)
    w_gate, w_proj = ple_w_gate.astype(BF16), ple_w_proj.astype(BF16)
    inner = SSM_GROUPS * SSM_GROUP_WIDTH
    for i in range(DEPTH):
        j = i // 2
        if i % 2 == 0:
            qkv = _qkv_mm(h, attn_norm[j], w_qkv, j, attn_q_norm[j], attn_k_norm[j], **TILES_QKV)
            o = _attention(qkv, batch, seq)
            h = _res_mm(o, w_o, j, h, **TILES_W_O)
        else:
            conv_w = jnp.pad(ssm_conv_w[j], ((0, 0), (inner, SSM_GROUPS * LANES)))
            bias = jnp.concatenate([jnp.zeros((1, inner), F32), ssm_conv_b[j].reshape(1, -1),
                                    _pad_heads(ssm_dt_bias[j])], axis=1)
            zx = _inproj(h, ssm_norm[j], w_in, w_dt, j, conv_w, bias, seq, **TILES_INPROJ)
            y = _ssd(zx, batch, seq, _pad_heads(ssm_a_log[j]),
                     jnp.repeat(ssm_d[j], SSM_HEAD_DIM).reshape(1, -1), ssm_out_norm[j])
            h = _res_mm(y, w_out, j, h, **TILES_SSM_OUT)
        u = _norm_mm(h, mlp_norm[i], w_up, i, out_dtype=BF16, **TILES_MLP_UP)
        h = _res_mm(u, w_down, i, h, **TILES_MLP_DOWN)
        h = _ple(h, p, i, ple_gate_norm[i], w_gate, w_proj, ple_norm[i], **TILES_PLE)
    return h.reshape(batch, seq, d)
```

```python
import functools

import jax
import jax.numpy as jnp
from jax import lax
from jax.experimental import pallas as pl
from jax.experimental.pallas import tpu as pltpu

F32 = jnp.float32
BF16 = jnp.bfloat16

NORM_EPS = 1e-6
LANES = 128
VMEM_LIMIT = 56 * 1024 * 1024

DEPTH = 4
PLE_DIM = 256
SB_HEADS = 16
SB_HEAD_DIM = 128
SSM_HEAD_DIM = 64
SSM_GROUPS = 8
SSM_HEADS_PER_GROUP = 8
SSM_STATE = 128
SSM_CONV = 4
SSM_GROUP_WIDTH = SSM_HEADS_PER_GROUP * SSM_HEAD_DIM
SSD_CHUNK = 128
SSD_GROUPS_PER_STEP = 8
ATTN_BLOCK = 256
ATTN_HEADS_PER_STEP = 8
LOG2E = 1.4426950408889634
NEG_BIG = -1e30


def _params(*sem):
    return pltpu.CompilerParams(dimension_semantics=sem, vmem_limit_bytes=VMEM_LIMIT)


def _rms(x, g):
    ms = jnp.mean(x * x, axis=-1, keepdims=True)
    return x * lax.rsqrt(ms + NORM_EPS) * g


def _softplus(x):
    return jnp.maximum(x, 0.0) + jnp.log(1.0 + jnp.exp(-jnp.abs(x)))


def _softplus2(x):
    neg_abs = lax.bitcast_convert_type(
        lax.bitcast_convert_type(x, jnp.uint32) | jnp.uint32(0x80000000), F32)
    return jnp.maximum(x, 0.0) + jnp.log2(1.0 + jnp.exp2(neg_abs))


def _silu(x):
    return x * jax.nn.sigmoid(x)


def _split2(x):
    hi = x.astype(BF16)
    lo = (x - hi.astype(F32)).astype(BF16)
    return hi, lo


def _split3(x):
    hi = x.astype(BF16)
    r1 = x - hi.astype(F32)
    mid = r1.astype(BF16)
    lo = (r1 - mid.astype(F32)).astype(BF16)
    return hi, mid, lo


def _dot(a, b):
    return jnp.dot(a, b, preferred_element_type=F32)


def _dot_nt(a, b):
    return lax.dot_general(a, b, (((1,), (1,)), ((), ())), preferred_element_type=F32)


def _dot_tn(a, b):
    return lax.dot_general(a, b, (((0,), (0,)), ((), ())), preferred_element_type=F32)


def _norm_mm_kernel(h_ref, g_ref, w_ref, o_ref, hn_ref):
    @pl.when(pl.program_id(1) == 0)
    def _():
        hn_ref[...] = _rms(h_ref[...], g_ref[...]).astype(BF16)

    r = jnp.maximum(_dot(hn_ref[...], w_ref[...]), 0.0)
    o_ref[...] = (r * r).astype(o_ref.dtype)


def _norm_mm(h, g, w, layer, *, out_dtype, tm, tn):
    m, d = h.shape
    n = w.shape[2]
    tm, tn = min(tm, m), min(tn, n)
    return pl.pallas_call(
        _norm_mm_kernel,
        grid=(m // tm, n // tn),
        in_specs=[
            pl.BlockSpec((tm, d), lambda i, j: (i, 0)),
            pl.BlockSpec((1, d), lambda i, j: (0, 0)),
            pl.BlockSpec((None, d, tn), lambda i, j: (layer, 0, j)),
        ],
        out_specs=pl.BlockSpec((tm, tn), lambda i, j: (i, j)),
        out_shape=jax.ShapeDtypeStruct((m, n), out_dtype),
        scratch_shapes=[pltpu.VMEM((tm, d), BF16)],
        compiler_params=_params("parallel", "arbitrary"),
        name="norm_mm_relu2",
    )(h, g.reshape(1, d), w)


def _qkv_kernel(h_ref, g_ref, w_ref, gq_ref, gk_ref, o_ref, hn_ref, *, tn, q_tiles, k_tiles):
    j = pl.program_id(1)

    @pl.when(j == 0)
    def _():
        hn_ref[...] = _rms(h_ref[...], g_ref[...]).astype(BF16)

    @pl.when(j < k_tiles)
    def _():
        acc = _dot(hn_ref[...], w_ref[...])
        gh = jnp.where(j < q_tiles, gq_ref[...], gk_ref[...])
        for hh in range(tn // SB_HEAD_DIM):
            sl = slice(hh * SB_HEAD_DIM, (hh + 1) * SB_HEAD_DIM)
            o_ref[:, sl] = _rms(acc[:, sl], gh).astype(BF16)

    @pl.when(j >= k_tiles)
    def _():
        o_ref[...] = _dot(hn_ref[...], w_ref[...]).astype(BF16)


def _qkv_mm(h, g, w, layer, gq, gk, *, tm, tn):
    m, d = h.shape
    n = w.shape[2]
    width = n // 3
    tm = min(tm, m)
    assert width % tn == 0
    return pl.pallas_call(
        functools.partial(_qkv_kernel, tn=tn, q_tiles=width // tn, k_tiles=2 * width // tn),
        grid=(m // tm, n // tn),
        in_specs=[
            pl.BlockSpec((tm, d), lambda i, j: (i, 0)),
            pl.BlockSpec((1, d), lambda i, j: (0, 0)),
            pl.BlockSpec((None, d, tn), lambda i, j: (layer, 0, j)),
            pl.BlockSpec((1, SB_HEAD_DIM), lambda i, j: (0, 0)),
            pl.BlockSpec((1, SB_HEAD_DIM), lambda i, j: (0, 0)),
        ],
        out_specs=pl.BlockSpec((tm, tn), lambda i, j: (i, j)),
        out_shape=jax.ShapeDtypeStruct((m, n), BF16),
        scratch_shapes=[pltpu.VMEM((tm, d), BF16)],
        compiler_params=_params("parallel", "arbitrary"),
        name="qkv_mm",
    )(h, g.reshape(1, d), w, gq.reshape(1, -1), gk.reshape(1, -1))


def _res_mm_kernel(a_ref, w_ref, r_ref, o_ref, *, nk):
    if nk == 1:
        o_ref[...] = r_ref[...] + _dot(a_ref[...], w_ref[...])
        return
    k = pl.program_id(2)

    @pl.when(k == 0)
    def _():
        o_ref[...] = _dot(a_ref[...], w_ref[...])

    @pl.when((k > 0) & (k < nk - 1))
    def _():
        o_ref[...] += _dot(a_ref[...], w_ref[...])

    @pl.when(k == nk - 1)
    def _():
        o_ref[...] = r_ref[...] + (o_ref[...] + _dot(a_ref[...], w_ref[...]))


def _res_mm(a, w, layer, res, *, tm, tn, tk):
    m, kdim = a.shape
    n = w.shape[2]
    tm, tn, tk = min(tm, m), min(tn, n), min(tk, kdim)
    nk = kdim // tk
    return pl.pallas_call(
        functools.partial(_res_mm_kernel, nk=nk),
        grid=(m // tm, n // tn, nk),
        in_specs=[
            pl.BlockSpec((tm, tk), lambda i, j, k: (i, k)),
            pl.BlockSpec((None, tk, tn), lambda i, j, k: (layer, k, j)),
            pl.BlockSpec((tm, tn), lambda i, j, k: (i, j)),
        ],
        out_specs=pl.BlockSpec((tm, tn), lambda i, j, k: (i, j)),
        out_shape=jax.ShapeDtypeStruct((m, n), F32),
        compiler_params=_params("parallel", "parallel", "arbitrary"),
        name="res_mm",
    )(a, w, res)


def _ple_kernel(h_ref, p_ref, gg_ref, wg_ref, wp_ref, gp_ref, o_ref, hn_ref, e_ref, *, tn):
    j = pl.program_id(1)

    def gated(col):
        gate = jax.nn.sigmoid(_dot(hn_ref[...], wg_ref[...]))
        o_ref[...] = h_ref[:, pl.ds(col, tn)] + gate * e_ref[:, pl.ds(col, tn)]

    @pl.when(j == 0)
    def _():
        hn_ref[...] = _rms(h_ref[...], gg_ref[...]).astype(BF16)
        e_ref[...] = _rms(_dot(p_ref[...].astype(BF16), wp_ref[...]), gp_ref[...])
        gated(0)

    @pl.when(j > 0)
    def _():
        gated(pl.multiple_of(j * tn, tn))


def _ple(h, p, layer, gg, wg, wp, gp, *, tm, tn):
    m, d = h.shape
    pd = p.shape[2]
    tm, tn = min(tm, m), min(tn, d)
    return pl.pallas_call(
        functools.partial(_ple_kernel, tn=tn),
        grid=(m // tm, d // tn),
        in_specs=[
            pl.BlockSpec((tm, d), lambda i, j: (i, 0)),
            pl.BlockSpec((None, tm, pd), lambda i, j: (layer, i, 0)),
            pl.BlockSpec((1, d), lambda i, j: (0, 0)),
            pl.BlockSpec((None, d, tn), lambda i, j: (layer, 0, j)),
            pl.BlockSpec((None, pd, d), lambda i, j: (layer, 0, 0)),
            pl.BlockSpec((1, d), lambda i, j: (0, 0)),
        ],
        out_specs=pl.BlockSpec((tm, tn), lambda i, j: (i, j)),
        out_shape=jax.ShapeDtypeStruct((m, d), F32),
        scratch_shapes=[pltpu.VMEM((tm, d), BF16), pltpu.VMEM((tm, d), F32)],
        compiler_params=_params("parallel", "arbitrary"),
        name="ple",
    )(h, p, gg.reshape(1, d), wg, wp, gp.reshape(1, d))


def _attn_kernel(q_ref, k_ref, v_ref, t_ref, o_ref, *, blk, heads, scale):
    qi = pl.program_id(2)
    dh = SB_HEAD_DIM
    tri2 = t_ref[...]
    qs = [q_ref[:, hh * dh:(hh + 1) * dh] for hh in range(heads)]

    def tiles(kb, carries, masked):
        start = pl.multiple_of(kb * blk, blk)
        hs = range(heads)
        ks = [k_ref[pl.ds(start, blk), hh * dh:(hh + 1) * dh] for hh in hs]
        zs = [_dot_nt(qs[hh], ks[hh]) * (scale * LOG2E) for hh in hs]
        sps = [_softplus2(z) for z in zs]
        spms = sps
        if masked:
            t_idx = lax.broadcasted_iota(jnp.int32, (blk, blk), 0)
            s_idx = lax.broadcasted_iota(jnp.int32, (blk, blk), 1)
            mask = s_idx < t_idx
            spms = [jnp.where(mask, sp, 0.0) for sp in sps]
        splits = [_split2(spm) for spm in spms]
        sufs = [_dot(jnp.concatenate([hi, lo], axis=1), tri2) for hi, lo in splits]
        a_s = [jnp.exp2((zs[hh] - sps[hh]) - sufs[hh] + carries[hh]) for hh in hs]
        if masked:
            a_s = [jnp.where(mask, a, 0.0) for a in a_s]
        vs = [v_ref[pl.ds(start, blk), hh * dh:(hh + 1) * dh] for hh in hs]
        pvs = [_dot(a_s[hh].astype(BF16), vs[hh]) for hh in hs]
        new_carries = [carries[hh] - jnp.sum(spms[hh], axis=-1, keepdims=True) for hh in hs]
        return pvs, new_carries

    accs, carries = tiles(qi, [jnp.zeros((blk, 1), F32)] * heads, True)

    def body(it, state):
        accs, carries = state[:heads], state[heads:]
        pvs, carries = tiles(qi - 1 - it, carries, False)
        return tuple(acc + pv for acc, pv in zip(accs, pvs)) + tuple(carries)

    state = lax.fori_loop(0, qi, body, tuple(accs) + tuple(carries))
    for hh in range(heads):
        o_ref[:, hh * dh:(hh + 1) * dh] = state[hh].astype(o_ref.dtype)


def _attention(qkv, batch, seq):
    blk = min(ATTN_BLOCK, seq)
    nq = seq // blk
    hg = SB_HEADS // ATTN_HEADS_PER_STEP
    gw = ATTN_HEADS_PER_STEP * SB_HEAD_DIM
    tri = (lax.broadcasted_iota(jnp.int32, (blk, blk), 0)
           > lax.broadcasted_iota(jnp.int32, (blk, blk), 1)).astype(BF16)
    tri2 = jnp.concatenate([tri, tri], axis=0)
    return pl.pallas_call(
        functools.partial(_attn_kernel, blk=blk, heads=ATTN_HEADS_PER_STEP, scale=SB_HEAD_DIM ** -0.5),
        grid=(batch, hg, nq),
        in_specs=[
            pl.BlockSpec((blk, gw), lambda b, g, i: (b * nq + i, g)),
            pl.BlockSpec((seq, gw), lambda b, g, i: (b, hg + g)),
            pl.BlockSpec((seq, gw), lambda b, g, i: (b, 2 * hg + g)),
            pl.BlockSpec((2 * blk, blk), lambda b, g, i: (0, 0)),
        ],
        out_specs=pl.BlockSpec((blk, gw), lambda b, g, i: (b * nq + i, g)),
        out_shape=jax.ShapeDtypeStruct((batch * seq, SB_HEADS * SB_HEAD_DIM), BF16),
        compiler_params=_params("parallel", "parallel", "arbitrary"),
        name="stickbreak_attn",
    )(qkv, qkv, qkv, tri2)


def _inproj_kernel(h_ref, g_ref, w_ref, wdt_ref, cw_ref, b_ref, o_ref, hn_ref, tail_ref, *,
                   tm, nj, z_tiles, tiles_per_seq):
    i, j = pl.program_id(0), pl.program_id(1)

    @pl.when(j == 0)
    def _():
        hn_ref[...] = _rms(h_ref[...], g_ref[...]).astype(BF16)

    @pl.when((j == 0) & (i % tiles_per_seq == 0))
    def _():
        tail_ref[...] = jnp.zeros_like(tail_ref)

    @pl.when(j < z_tiles)
    def _():
        o_ref[...] = _silu(_dot(hn_ref[...], w_ref[...]))

    @pl.when((j >= z_tiles) & (j < nj - 1))
    def _():
        acc = _dot(hn_ref[...], w_ref[...])
        ext = jnp.concatenate([tail_ref[j - z_tiles], acc], axis=0)
        out = acc * cw_ref[SSM_CONV - 1:SSM_CONV, :] + b_ref[...]
        for s in range(1, SSM_CONV):
            out = out + ext[8 - s:8 - s + tm, :] * cw_ref[SSM_CONV - 1 - s:SSM_CONV - s, :]
        o_ref[...] = _silu(out)
        tail_ref[j - z_tiles] = acc[tm - 8:tm, :]

    @pl.when(j == nj - 1)
    def _():
        o_ref[...] = _softplus(_dot(hn_ref[...], wdt_ref[...]) + b_ref[...])


def _inproj(h, g, w, w_dt, layer, conv_w, bias, seq, *, tm, tn):
    m, d = h.shape
    tm = min(tm, m, seq)
    inner = SSM_GROUPS * SSM_GROUP_WIDTH
    n = 2 * inner + 2 * SSM_GROUPS * SSM_STATE + tn
    ni, nj = m // tm, n // tn
    assert seq % tm == 0 and inner % tn == 0 and n % tn == 0 and w_dt.shape[2] == tn
    z_tiles = inner // tn
    kern = functools.partial(_inproj_kernel, tm=tm, nj=nj, z_tiles=z_tiles, tiles_per_seq=seq // tm)
    return pl.pallas_call(
        kern,
        grid=(ni, nj),
        in_specs=[
            pl.BlockSpec((tm, d), lambda i, j: (i, 0)),
            pl.BlockSpec((1, d), lambda i, j: (0, 0)),
            pl.BlockSpec((None, d, tn), lambda i, j: (layer, 0, jnp.minimum(j, nj - 2))),
            pl.BlockSpec((None, d, tn), lambda i, j: (layer, 0, 0)),
            pl.BlockSpec((SSM_CONV, tn), lambda i, j: (0, j)),
            pl.BlockSpec((1, tn), lambda i, j: (0, j)),
        ],
        out_specs=pl.BlockSpec((tm, tn), lambda i, j: (i, j)),
        out_shape=jax.ShapeDtypeStruct((m, n), F32),
        scratch_shapes=[pltpu.VMEM((tm, d), BF16),
                        pltpu.VMEM((nj - 1 - z_tiles, 8, tn), F32)],
        compiler_params=_params("arbitrary", "arbitrary"),
        name="ssm_inproj",
    )(h, g.reshape(1, d), w, w_dt, conv_w, bias)


def _ssd_kernel(z_ref, x_ref, b_ref, c_ref, dt_ref, alog_ref, d_ref, g_ref, tri_ref, exp_ref,
                o_ref, state_ref, *, q, groups):
    @pl.when(pl.program_id(2) == 0)
    def _():
        state_ref[...] = jnp.zeros_like(state_ref)

    gw, ns, gs = SSM_GROUP_WIDTH, SSM_STATE, range(groups)
    pairs = range(SSM_HEADS_PER_GROUP // 2)
    tri = tri_ref[...]
    expand2 = exp_ref[...]
    i_idx = lax.broadcasted_iota(jnp.int32, (q, q), 0)
    j_idx = lax.broadcasted_iota(jnp.int32, (q, q), 1)
    causal = j_idx <= i_idx
    left2 = lax.broadcasted_iota(jnp.int32, (2 * q, LANES), 1) < SSM_HEAD_DIM
    top2 = lax.broadcasted_iota(jnp.int32, (2 * q, LANES), 0) < q
    block_diag = left2 == top2

    def expand(v):
        hi, lo = _split2(v)
        return _dot(jnp.concatenate([hi, lo], axis=1), expand2)

    xs = [x_ref[:, g * gw:(g + 1) * gw] for g in gs]
    bm16 = [b_ref[:, g * ns:(g + 1) * ns].astype(BF16) for g in gs]
    cm16 = [c_ref[:, g * ns:(g + 1) * ns].astype(BF16) for g in gs]
    dts = [dt_ref[:, g * LANES:(g + 1) * LANES] for g in gs]
    das = [dts[g] * (-jnp.exp(alog_ref[:, g * LANES:(g + 1) * LANES])) for g in gs]
    splits = [_split3(da) for da in das]
    acums = [_dot(tri, hi) + _dot(tri, mid) + _dot(tri, lo) for hi, mid, lo in splits]
    states = [state_ref[g] for g in gs]
    cbs = [_dot_nt(cm16[g], bm16[g]) for g in gs]
    y_offs = [_dot(cm16[g], states[g].astype(BF16)) for g in gs]
    acum_ts = [acum.T for acum in acums]
    dt_ts = [dt.T for dt in dts]
    grows = [expand(jnp.exp(acum)) for acum in acums]
    tails = [expand(jnp.exp(acums[g][q - 1:q, :] - acums[g]) * dts[g]) for g in gs]

    y_diags = [[None] * len(pairs) for g in gs]
    for pair in pairs:
        sl = slice(pair * LANES, (pair + 1) * LANES)
        m_cats, x_bds = [], []
        for g in gs:
            mats = []
            for hh in (2 * pair, 2 * pair + 1):
                diff = acums[g][:, hh:hh + 1] - acum_ts[g][hh:hh + 1, :]
                decay = jnp.exp(jnp.where(causal, diff, NEG_BIG))
                mats.append((cbs[g] * decay * dt_ts[g][hh:hh + 1, :]).astype(BF16))
            m_cats.append(jnp.concatenate(mats, axis=1))
            xp = xs[g][:, sl]
            x_bds.append(jnp.where(block_diag, jnp.concatenate([xp, xp], axis=0), 0.0).astype(BF16))
        for g in gs:
            y_diags[g][pair] = _dot(m_cats[g], x_bds[g])

    xws = [(xs[g] * tails[g]).astype(BF16) for g in gs]
    updates = [_dot_tn(bm16[g], xws[g]) for g in gs]
    for g in gs:
        wide = slice(g * gw, (g + 1) * gw)
        state_ref[g] = states[g] * grows[g][q - 1:q, :] + updates[g]
        y = jnp.concatenate(y_diags[g], axis=1) + y_offs[g] * grows[g] + d_ref[:, wide] * xs[g]
        o_ref[:, wide] = _rms(y * z_ref[:, wide], g_ref[:, wide]).astype(o_ref.dtype)


def _ssd(zx, batch, seq, a_log, d_skip, g_out):
    q = min(SSD_CHUNK, seq)
    nc = seq // q
    gs = SSD_GROUPS_PER_STEP
    gw, ns, g = gs * SSM_GROUP_WIDTH, gs * SSM_STATE, SSM_GROUPS
    inner = SSM_GROUPS * SSM_GROUP_WIDTH
    x0, b0 = inner // gw, 2 * inner // ns
    c0, dt0 = b0 + g // gs, b0 + 2 * (g // gs)

    def row(b, gg, c):
        return b * nc + c

    tri = (lax.broadcasted_iota(jnp.int32, (q, q), 1)
           <= lax.broadcasted_iota(jnp.int32, (q, q), 0)).astype(BF16)
    expand = (lax.broadcasted_iota(jnp.int32, (LANES, SSM_GROUP_WIDTH), 0)
              == lax.broadcasted_iota(jnp.int32, (LANES, SSM_GROUP_WIDTH), 1) // SSM_HEAD_DIM).astype(BF16)
    expand2 = jnp.concatenate([expand, expand], axis=0)
    in_specs = [
        pl.BlockSpec((q, gw), lambda b, gg, c: (row(b, gg, c), gg)),
        pl.BlockSpec((q, gw), lambda b, gg, c: (row(b, gg, c), x0 + gg)),
        pl.BlockSpec((q, ns), lambda b, gg, c: (row(b, gg, c), b0 + gg)),
        pl.BlockSpec((q, ns), lambda b, gg, c: (row(b, gg, c), c0 + gg)),
        pl.BlockSpec((q, gs * LANES), lambda b, gg, c: (row(b, gg, c), dt0 + gg)),
        pl.BlockSpec((1, gs * LANES), lambda b, gg, c: (0, gg)),
        pl.BlockSpec((1, gw), lambda b, gg, c: (0, gg)),
        pl.BlockSpec((1, gw), lambda b, gg, c: (0, gg)),
        pl.BlockSpec((q, q), lambda b, gg, c: (0, 0)),
        pl.BlockSpec((2 * LANES, SSM_GROUP_WIDTH), lambda b, gg, c: (0, 0)),
    ]
    return pl.pallas_call(
        functools.partial(_ssd_kernel, q=q, groups=gs),
        grid=(batch, g // gs, nc),
        in_specs=in_specs,
        out_specs=pl.BlockSpec((q, gw), lambda b, gg, c: (row(b, gg, c), gg)),
        out_shape=jax.ShapeDtypeStruct((batch * seq, inner), BF16),
        scratch_shapes=[pltpu.VMEM((gs, SSM_STATE, SSM_GROUP_WIDTH), F32)],
        compiler_params=_params("parallel", "parallel", "arbitrary"),
        name="ssd",
    )(zx, zx, zx, zx, zx, a_log, d_skip, g_out.reshape(1, -1), tri, expand2)


def _pad_heads(v):
    v = v.reshape(SSM_GROUPS, SSM_HEADS_PER_GROUP)
    v = jnp.pad(v, ((0, 0), (0, LANES - SSM_HEADS_PER_GROUP)))
    return v.reshape(1, SSM_GROUPS * LANES)


def _dt_columns(w_in):
    lead = w_in.shape[:-1]
    heads = SSM_GROUPS * SSM_HEADS_PER_GROUP
    w_dt = w_in[..., -heads:].reshape(*lead, SSM_GROUPS, SSM_HEADS_PER_GROUP)
    w_dt = jnp.pad(w_dt, [(0, 0)] * (len(lead) + 1) + [(0, LANES - SSM_HEADS_PER_GROUP)])
    return w_dt.reshape(*lead, SSM_GROUPS * LANES)


TILES_INPROJ = dict(tm=1024, tn=1024)
TILES_QKV = dict(tm=1024, tn=2048)
TILES_MLP_UP = dict(tm=1024, tn=2048)
TILES_W_O = dict(tm=512, tn=2048, tk=2048)
TILES_SSM_OUT = dict(tm=1024, tn=1024, tk=2048)
TILES_MLP_DOWN = dict(tm=1024, tn=1024, tk=2048)
TILES_PLE = dict(tm=1024, tn=512)


def kernel(x, p, attn_norm, attn_w_qkv, attn_q_norm, attn_k_norm, attn_w_o, ssm_norm, ssm_w_in, ssm_conv_w, ssm_conv_b, ssm_dt_bias, ssm_a_log, ssm_d, ssm_out_norm, ssm_w_out, mlp_norm, mlp_w_up, mlp_w_down, ple_w_proj, ple_norm, ple_gate_norm, ple_w_gate):
    batch, seq, d = x.shape
    m = batch * seq
    h = x.reshape(m, d)
    p = p.reshape(p.shape[0], m, PLE_DIM)
    w_qkv, w_o = attn_w_qkv.astype(BF16), attn_w_o.astype(BF16)
    conv_end = 2 * (SSM_GROUPS * SSM_GROUP_WIDTH + SSM_GROUPS * SSM_STATE)
    w_in, w_out = ssm_w_in[..., :conv_end].astype(BF16), ssm_w_out.astype(BF16)
    w_dt = _dt_columns(ssm_w_in).astype(BF16)
    w_up, w_down = mlp_w_up.astype(BF16), mlp_w_down.astype(BF16)
    w_gate, w_proj = ple_w_gate.astype(BF16), ple_w_proj.astype(BF16)
    inner = SSM_GROUPS * SSM_GROUP_WIDTH
    for i in range(DEPTH):
        j = i // 2
        if i % 2 == 0:
            qkv = _qkv_mm(h, attn_norm[j], w_qkv, j, attn_q_norm[j], attn_k_norm[j], **TILES_QKV)
            o = _attention(qkv, batch, seq)
            h = _res_mm(o, w_o, j, h, **TILES_W_O)
        else:
            conv_w = jnp.pad(ssm_conv_w[j], ((0, 0), (inner, SSM_GROUPS * LANES)))
            bias = jnp.concatenate([jnp.zeros((1, inner), F32), ssm_conv_b[j].reshape(1, -1),
                                    _pad_heads(ssm_dt_bias[j])], axis=1)
            zx = _inproj(h, ssm_norm[j], w_in, w_dt, j, conv_w, bias, seq, **TILES_INPROJ)
            y = _ssd(zx, batch, seq, _pad_heads(ssm_a_log[j]),
                     jnp.repeat(ssm_d[j], SSM_HEAD_DIM).reshape(1, -1), ssm_out_norm[j])
            h = _res_mm(y, w_out, j, h, **TILES_SSM_OUT)
        u = _norm_mm(h, mlp_norm[i], w_up, i, out_dtype=BF16, **TILES_MLP_UP)
        h = _res_mm(u, w_down, i, h, **TILES_MLP_DOWN)
        h = _ple(h, p, i, ple_gate_norm[i], w_gate, w_proj, ple_norm[i], **TILES_PLE)
    return h.reshape(batch, seq, d)
```
